```python
import jax, jax.numpy as jnp
from jax import lax
import numpy as np

D_MODEL = 1024
BATCH = 8
SEQ = 8192
DEPTH = 4

N_META = 16
N_MIXERS = 2
SB_HEADS = 8
SB_HEAD_DIM = D_MODEL // SB_HEADS
Q_BLOCK = 128
SUB_BLOCK = 32
POOL_WINDOWS = (2, 4, 8, 16)
POOL_GROUPS = len(POOL_WINDOWS)
POOL_GROUP_DIM = D_MODEL // POOL_GROUPS
MOE_GROUPS = 4
EXPERTS_PER_GROUP = 8
N_EXPERTS = MOE_GROUPS * EXPERTS_PER_GROUP
EXPERT_TOP_K = 2
EXPERT_FF = D_MODEL // 4
MOE_BLOCK = 256
RMS_EPS = 1e-6
N_SB_LAYERS = (DEPTH + 1) // 2
N_POOL_LAYERS = DEPTH // 2

kernel_name = "hybrid_stickbreak_pool_hmoe"


def rms_norm(x, g):
    xf = x.astype(jnp.float32)
    y = xf * lax.rsqrt(jnp.mean(xf * xf, axis=-1, keepdims=True) + RMS_EPS)
    return (y * g.astype(jnp.float32)).astype(x.dtype)


def stick_breaking_attention(h, w_qkv, w_o):
    b, l, _ = h.shape
    n_pad = Q_BLOCK - N_META
    qkv = jnp.einsum('bld,de->ble', h, w_qkv).reshape(b, l, 3, SB_HEADS, SB_HEAD_DIM)
    qkv = jnp.pad(qkv, ((0, 0), (n_pad, 0), (0, 0), (0, 0), (0, 0)))
    p = l + n_pad
    n_blocks = p // Q_BLOCK
    q = jnp.transpose(qkv[:, :, 0], (0, 2, 1, 3))
    k = jnp.transpose(qkv[:, :, 1], (0, 2, 1, 3))
    v = jnp.transpose(qkv[:, :, 2], (0, 2, 1, 3))
    scale = SB_HEAD_DIM ** -0.5
    sub_idx = jnp.arange(SUB_BLOCK)
    sub_suffix = (sub_idx[:, None] > sub_idx[None, :]).astype(jnp.float32)
    outs = []
    for i in range(n_blocks):
        kl = (i + 1) * Q_BLOCK
        n_sub = kl // SUB_BLOCK
        q_blk = q[:, :, i * Q_BLOCK:(i + 1) * Q_BLOCK]
        k_blk = k[:, :, :kl]
        v_blk = v[:, :, :kl]
        q_pos = i * Q_BLOCK + jnp.arange(Q_BLOCK)
        key_pos = jnp.arange(kl)
        mask = (key_pos[None, :] < q_pos[:, None]) & (key_pos >= n_pad)[None, :]
        z = jnp.einsum('bhqd,bhkd->bhqk', q_blk, k_blk, preferred_element_type=jnp.float32) * scale
        log_rest = jnp.where(mask, jax.nn.log_sigmoid(-z), 0.0)
        lr = log_rest.reshape(b, SB_HEADS, Q_BLOCK, n_sub, SUB_BLOCK)
        within = jnp.einsum('bhqcj,js->bhqcs', lr, sub_suffix)
        blk_idx = jnp.arange(n_sub)
        blk_suffix = (blk_idx[:, None] > blk_idx[None, :]).astype(jnp.float32)
        after = jnp.einsum('bhqc,cd->bhqd', jnp.sum(lr, axis=-1), blk_suffix)
        later = (within + after[..., None]).reshape(b, SB_HEADS, Q_BLOCK, kl)
        a = jnp.exp(jnp.where(mask, z + log_rest + later, -jnp.inf))
        outs.append(jnp.einsum('bhqk,bhkd->bhqd', a.astype(v.dtype), v_blk))
    o = jnp.concatenate(outs, axis=2)
    o = jnp.transpose(o, (0, 2, 1, 3)).reshape(b, p, SB_HEADS * SB_HEAD_DIM)[:, n_pad:]
    return jnp.einsum('ble,ed->bld', o, w_o)


def multiscale_pool_mixer(h, w_pool, pool_scale):
    b, l, d = h.shape
    hf = h.astype(jnp.float32)
    zero = jnp.zeros((), jnp.float32)
    groups = []
    for gi, w in enumerate(POOL_WINDOWS):
        xg = hf[:, :, gi * POOL_GROUP_DIM:(gi + 1) * POOL_GROUP_DIM]
        xg_pad = jnp.pad(xg, ((0, 0), (w - 1, 0), (0, 0)))
        wsum = lax.reduce_window(xg_pad, zero, lax.add, (1, w, 1), (1, 1, 1), 'VALID')
        count = jnp.minimum(jnp.arange(1, l + 1), w).astype(jnp.float32)
        groups.append(wsum / count[None, :, None] - xg)
    mixed = jnp.stack(groups, axis=2).astype(h.dtype)
    y = jnp.einsum('blgc,gce->blge', mixed, w_pool).reshape(b, l, d)
    return (y * pool_scale).astype(h.dtype)


def routed_experts(xf, expert_ids, gates, w_gate, w_up, w_down):
    t, d = xf.shape
    n = t * EXPERT_TOP_K
    flat_e = expert_ids.reshape(n)
    flat_tok = jnp.repeat(jnp.arange(t), EXPERT_TOP_K)
    flat_gate = gates.reshape(n)
    order = jnp.argsort(flat_e)
    sorted_e = flat_e[order]
    counts = jnp.bincount(flat_e, length=N_EXPERTS)
    starts = jnp.cumsum(counts) - counts
    padded_counts = (counts + MOE_BLOCK - 1) // MOE_BLOCK * MOE_BLOCK
    padded_ends = jnp.cumsum(padded_counts)
    padded_starts = padded_ends - padded_counts
    dest = padded_starts[sorted_e] + jnp.arange(n) - starts[sorted_e]
    n_blocks = (n + N_EXPERTS * (MOE_BLOCK - 1)) // MOE_BLOCK
    n_rows = n_blocks * MOE_BLOCK
    row_tok = jnp.full((n_rows,), t, jnp.int32).at[dest].set(flat_tok[order])
    row_gate = jnp.zeros((n_rows,), jnp.float32).at[dest].set(flat_gate[order])
    block_expert = jnp.minimum(
        jnp.searchsorted(padded_ends, jnp.arange(n_blocks) * MOE_BLOCK, side='right'), N_EXPERTS - 1)
    x_rows = jnp.concatenate([xf, jnp.zeros((1, d), xf.dtype)], axis=0)[row_tok]
    x_rows = x_rows.reshape(n_blocks, MOE_BLOCK, d)

    def expert_block(args):
        xb, e = args
        hid = jax.nn.silu(xb @ w_gate[e]) * (xb @ w_up[e])
        return hid @ w_down[e]

    y_rows = lax.map(expert_block, (x_rows, block_expert)).reshape(n_rows, d)
    y = jnp.zeros((t + 1, d), jnp.float32).at[row_tok].add(y_rows.astype(jnp.float32) * row_gate[:, None])
    return y[:t]


def hierarchical_moe(h, wg, bg, we, be, w_gate, w_up, w_down):
    b, l, d = h.shape
    t = b * l
    xf = h.reshape(t, d)
    g_logits = jnp.einsum('td,dg->tg', xf, wg, preferred_element_type=jnp.float32) + bg.astype(jnp.float32)
    p_group = jax.nn.softmax(g_logits, axis=-1)
    g_sel = jnp.argmax(p_group, axis=-1).astype(jnp.int32)
    g_prob = jnp.max(p_group, axis=-1)
    e_logits = jnp.einsum('td,de->te', xf, we, preferred_element_type=jnp.float32) + be.astype(jnp.float32)
    e_logits = e_logits.reshape(t, MOE_GROUPS, EXPERTS_PER_GROUP)
    e_in_group = jnp.take_along_axis(e_logits, g_sel[:, None, None], axis=1)[:, 0]
    p_in = jax.nn.softmax(e_in_group, axis=-1)
    top_p, top_i = lax.top_k(p_in, EXPERT_TOP_K)
    gates = g_prob[:, None] * top_p / jnp.sum(top_p, axis=-1, keepdims=True)
    expert_ids = g_sel[:, None] * EXPERTS_PER_GROUP + top_i.astype(jnp.int32)
    y = routed_experts(xf, expert_ids, gates, w_gate, w_up, w_down)
    return y.reshape(b, l, d).astype(h.dtype)


def setup_inputs(seed: int = 0) -> dict:
    key = jax.random.key(seed)
    ks = jax.random.split(key, 16)
    f32 = jnp.float32
    nrm = lambda k, shape, s: jax.random.normal(k, shape, f32) * s
    return {
        "x": nrm(ks[0], (BATCH, SEQ, D_MODEL), 1.0),
        "meta_tokens": nrm(ks[1], (N_META, D_MODEL), 1.0),
        "mix_norm": 1.0 + nrm(ks[2], (DEPTH, D_MODEL), 0.02),
        "ffn_norm": 1.0 + nrm(ks[3], (DEPTH, D_MODEL), 0.02),
        "final_norm": 1.0 + nrm(ks[4], (D_MODEL,), 0.02),
        "sb_w_qkv": nrm(ks[5], (N_SB_LAYERS, D_MODEL, 3 * SB_HEADS * SB_HEAD_DIM), D_MODEL ** -0.5),
        "sb_w_o": nrm(ks[6], (N_SB_LAYERS, SB_HEADS * SB_HEAD_DIM, D_MODEL), (SB_HEADS * SB_HEAD_DIM) ** -0.5),
        "pool_w": nrm(ks[7], (N_POOL_LAYERS, POOL_GROUPS, POOL_GROUP_DIM, POOL_GROUP_DIM), POOL_GROUP_DIM ** -0.5),
        "pool_scale": 1.0 + nrm(ks[8], (N_POOL_LAYERS, D_MODEL), 0.02),
        "router_group_w": nrm(ks[9], (DEPTH, D_MODEL, MOE_GROUPS), D_MODEL ** -0.5),
        "router_group_b": nrm(ks[10], (DEPTH, MOE_GROUPS), 0.01),
        "router_expert_w": nrm(ks[11], (DEPTH, D_MODEL, N_EXPERTS), D_MODEL ** -0.5),
        "router_expert_b": nrm(ks[12], (DEPTH, N_EXPERTS), 0.01),
        "expert_w_gate": nrm(ks[13], (DEPTH, N_EXPERTS, D_MODEL, EXPERT_FF), D_MODEL ** -0.5),
        "expert_w_up": nrm(ks[14], (DEPTH, N_EXPERTS, D_MODEL, EXPERT_FF), D_MODEL ** -0.5),
        "expert_w_down": nrm(ks[15], (DEPTH, N_EXPERTS, EXPERT_FF, D_MODEL), EXPERT_FF ** -0.5),
    }


def reference(x, meta_tokens, mix_norm, ffn_norm, final_norm, sb_w_qkv, sb_w_o, pool_w, pool_scale,
              router_group_w, router_group_b, router_expert_w, router_expert_b,
              expert_w_gate, expert_w_up, expert_w_down):
    b = x.shape[0]
    meta = jnp.broadcast_to(meta_tokens[None].astype(x.dtype), (b, N_META, x.shape[-1]))
    h = jnp.concatenate([meta, x], axis=1)
    for i in range(DEPTH):
        hn = rms_norm(h, mix_norm[i])
        j = i // N_MIXERS
        if i % N_MIXERS == 0:
            h = h + stick_breaking_attention(hn, sb_w_qkv[j], sb_w_o[j])
        else:
            h = h + multiscale_pool_mixer(hn, pool_w[j], pool_scale[j])
        h = h + hierarchical_moe(rms_norm(h, ffn_norm[i]), router_group_w[i], router_group_b[i],
                                 router_expert_w[i], router_expert_b[i],
                                 expert_w_gate[i], expert_w_up[i], expert_w_down[i])
    return rms_norm(h, final_norm)[:, N_META:]
```

```python
import functools

import numpy as np
import jax
import jax.numpy as jnp
from jax import lax
from jax.experimental import pallas as pl
from jax.experimental.pallas import tpu as pltpu

D_MODEL = 1024
N_META = 16
HEADS = 8
HEAD_DIM = D_MODEL // HEADS
POOL_WINDOWS = (2, 4, 8, 16)
POOL_GROUP_DIM = D_MODEL // len(POOL_WINDOWS)
POOL_HALO = 16
MOE_GROUPS = 4
EXPERTS_PER_GROUP = 8
N_EXPERTS = MOE_GROUPS * EXPERTS_PER_GROUP
EXPERT_FF = D_MODEL // 4
PAIRS_PER_GROUP = EXPERTS_PER_GROUP * (EXPERTS_PER_GROUP - 1) // 2
N_CLASSES = MOE_GROUPS * PAIRS_PER_GROUP
RMS_EPS = 1e-6

LANES = 128
ATTN_TILE = 256
FRONT = ATTN_TILE - N_META
ROW_BLOCK = 256
XG_WIDTH = D_MODEL + LANES
GATE_LO_LANE, GATE_HI_LANE, CLASS_LANE, RANK_LANE = 0, 1, 2, 3
NEG_BIG = -1e30
VMEM_LIMIT = 48 * 1024 * 1024

_F32 = jnp.float32
_BF16 = jnp.bfloat16


def _class_tables():
    lo, hi = [], []
    for g in range(MOE_GROUPS):
        for a in range(EXPERTS_PER_GROUP):
            for b in range(a + 1, EXPERTS_PER_GROUP):
                lo.append(g * EXPERTS_PER_GROUP + a)
                hi.append(g * EXPERTS_PER_GROUP + b)
    return np.asarray(lo, np.int32), np.asarray(hi, np.int32)


_CLASS_LO, _CLASS_HI = _class_tables()


def _pick_tile(n, candidates):
    for c in candidates:
        if n % c == 0:
            return c
    raise ValueError(f"no tile in {candidates} divides {n}")


def _params(*semantics):
    return pltpu.CompilerParams(dimension_semantics=semantics, vmem_limit_bytes=VMEM_LIMIT)


def _rms(x, g):
    return x * lax.rsqrt(jnp.mean(x * x, axis=-1, keepdims=True) + RMS_EPS) * g


def _qkv_kernel(h_ref, g_ref, w_ref, q_ref, k_ref, v_ref):
    xn = _rms(h_ref[...], g_ref[...]).astype(_BF16)
    scale = HEAD_DIM ** -0.5
    q = jnp.dot(xn, w_ref[:, 0:D_MODEL], preferred_element_type=_F32)
    q_ref[...] = (q * scale).astype(q_ref.dtype)
    k = jnp.dot(xn, w_ref[:, D_MODEL:2 * D_MODEL], preferred_element_type=_F32)
    k_ref[...] = k.astype(k_ref.dtype)
    v = jnp.dot(xn, w_ref[:, 2 * D_MODEL:3 * D_MODEL], preferred_element_type=_F32)
    v_ref[...] = v.astype(v_ref.dtype)


def _qkv(h2, gain, w_qkv):
    t = h2.shape[0]
    tm = _pick_tile(t, (512, 256))
    row = pl.BlockSpec((tm, D_MODEL), lambda i: (i, 0))
    out = jax.ShapeDtypeStruct((t, D_MODEL), _BF16)
    return pl.pallas_call(
        _qkv_kernel,
        grid=(t // tm,),
        in_specs=[row, pl.BlockSpec((1, D_MODEL), lambda i: (0, 0)),
                  pl.BlockSpec((D_MODEL, 3 * D_MODEL), lambda i: (0, 0))],
        out_specs=[row, row, row],
        out_shape=[out, out, out],
        compiler_params=_params("parallel"),
        name="qkv_proj",
    )(h2, gain, w_qkv)


def _attn_kernel(q_ref, k_ref, v_ref, o_ref, acc_ref, carry_ref):
    qi = pl.program_id(2)
    q = q_ref[0]
    tri_r = lax.broadcasted_iota(jnp.int32, (ATTN_TILE, ATTN_TILE), 0)
    tri_c = lax.broadcasted_iota(jnp.int32, (ATTN_TILE, ATTN_TILE), 1)
    suffix = (tri_r > tri_c).astype(_BF16)
    acc_ref[...] = jnp.zeros_like(acc_ref)
    carry_ref[...] = jnp.zeros_like(carry_ref)

    def step(kb, masked):
        start = pl.multiple_of(kb * ATTN_TILE, ATTN_TILE)
        k = k_ref[0, pl.ds(start, ATTN_TILE), :]
        v = v_ref[0, pl.ds(start, ATTN_TILE), :]
        z = lax.dot_general(q, k, (((1,), (1,)), ((), ())), preferred_element_type=_F32)
        log_rest = -(jnp.maximum(z, 0.0) + jnp.log(1.0 + jnp.exp(-jnp.abs(z))))
        if masked:
            q_pos = qi * ATTN_TILE + tri_r
            k_pos = kb * ATTN_TILE + tri_c
            mask = (k_pos < q_pos) & (k_pos >= FRONT)
            log_rest = jnp.where(mask, log_rest, 0.0)
        within = jnp.dot(log_rest.astype(_BF16), suffix, preferred_element_type=_F32)
        later = within + carry_ref[...]
        a = jnp.exp(z + log_rest + later)
        if masked:
            a = jnp.where(mask, a, 0.0)
        acc_ref[...] += jnp.dot(a.astype(_BF16), v, preferred_element_type=_F32)
        carry_ref[...] = later[:, 0:1] + log_rest[:, 0:1]

    step(qi, True)

    def body(j, c):
        step(qi - 1 - j, False)
        return c

    lax.fori_loop(0, jnp.maximum(qi - 1, 0), body, 0)

    @pl.when(qi > 0)
    def _():
        step(0, True)

    o_ref[0] = acc_ref[...].astype(o_ref.dtype)


def _attention(q, k, v):
    b, p, _ = q.shape
    nq = p // ATTN_TILE
    qspec = pl.BlockSpec((1, ATTN_TILE, HEAD_DIM), lambda bi, hi, qi: (bi, qi, hi))
    kvspec = pl.BlockSpec((1, p, HEAD_DIM), lambda bi, hi, qi: (bi, 0, hi))
    return pl.pallas_call(
        _attn_kernel,
        grid=(b, HEADS, nq),
        in_specs=[qspec, kvspec, kvspec],
        out_specs=qspec,
        out_shape=jax.ShapeDtypeStruct(q.shape, _BF16),
        scratch_shapes=[pltpu.VMEM((ATTN_TILE, HEAD_DIM), _F32), pltpu.VMEM((ATTN_TILE, 1), _F32)],
        compiler_params=_params("parallel", "parallel", "arbitrary"),
        name="sb_attention",
    )(q, k, v)


def _route(h_new, g_ref, wrh_ref, wrl_ref, br_ref, cnt_ref, xg_ref, counts_ref):
    tm = h_new.shape[0]
    xn = _rms(h_new, g_ref[...])
    x_hi = xn.astype(_BF16)
    x_lo = (xn - x_hi.astype(_F32)).astype(_BF16)
    logits = (jnp.dot(x_hi, wrh_ref[...], preferred_element_type=_F32)
              + jnp.dot(x_lo, wrh_ref[...], preferred_element_type=_F32)
              + jnp.dot(x_hi, wrl_ref[...], preferred_element_type=_F32)) + br_ref[...]
    lane = lax.broadcasted_iota(jnp.int32, (tm, LANES), 1)

    def first_argmax(vals, vmax):
        return jnp.min(jnp.where(vals >= vmax, lane, LANES), axis=-1, keepdims=True)

    gl = jnp.where(lane < MOE_GROUPS, logits, NEG_BIG)
    gmax = jnp.max(gl, axis=-1, keepdims=True)
    gsel = first_argmax(gl, gmax)
    gprob = 1.0 / jnp.sum(jnp.exp(gl - gmax), axis=-1, keepdims=True)
    first = MOE_GROUPS + EXPERTS_PER_GROUP * gsel
    el = jnp.where((lane >= first) & (lane < first + EXPERTS_PER_GROUP), logits, NEG_BIG)
    m1 = jnp.max(el, axis=-1, keepdims=True)
    i1 = first_argmax(el, m1)
    el2 = jnp.where(lane == i1, NEG_BIG, el)
    m2 = jnp.max(el2, axis=-1, keepdims=True)
    i2 = first_argmax(el2, m2)
    ratio = jnp.exp(m2 - m1)
    gate1 = gprob / (1.0 + ratio)
    gate2 = gprob * ratio / (1.0 + ratio)
    a = jnp.minimum(i1, i2) - first
    bb = jnp.maximum(i1, i2) - first
    pair = (EXPERTS_PER_GROUP - 1) * a - jnp.right_shift(a * (a - 1), 1) + (bb - a - 1)
    cls = gsel * PAIRS_PER_GROUP + pair
    gate_lo = jnp.where(i1 < i2, gate1, gate2)
    gate_hi = jnp.where(i1 < i2, gate2, gate1)

    onehot = (lane == cls).astype(_F32)
    tri_r = lax.broadcasted_iota(jnp.int32, (tm, tm), 0)
    tri_c = lax.broadcasted_iota(jnp.int32, (tm, tm), 1)
    before = (tri_c < tri_r).astype(_BF16)
    seen = jnp.dot(before, onehot.astype(_BF16), preferred_element_type=_F32) + cnt_ref[...]
    rank = jnp.sum(onehot * seen, axis=-1, keepdims=True)
    cnt_ref[...] += jnp.sum(onehot, axis=0, keepdims=True)
    counts_ref[...] = cnt_ref[...]

    extra = jnp.where(lane == GATE_LO_LANE, gate_lo,
                      jnp.where(lane == GATE_HI_LANE, gate_hi,
                                jnp.where(lane == CLASS_LANE, cls.astype(_F32),
                                          jnp.where(lane == RANK_LANE, rank, 0.0))))
    xg_ref[:, 0:D_MODEL] = xn
    xg_ref[:, D_MODEL:XG_WIDTH] = extra


def _router_weights(wg, bg, we, be):
    w = jnp.zeros((D_MODEL, LANES), _F32)
    w = w.at[:, 0:MOE_GROUPS].set(wg).at[:, MOE_GROUPS:MOE_GROUPS + N_EXPERTS].set(we)
    bias = jnp.zeros((1, LANES), _F32)
    bias = bias.at[0, 0:MOE_GROUPS].set(bg).at[0, MOE_GROUPS:MOE_GROUPS + N_EXPERTS].set(be)
    w_hi = w.astype(_BF16)
    w_lo = (w - w_hi.astype(_F32)).astype(_BF16)
    return w_hi, w_lo, bias


def _post_attn_kernel(o_ref, h_ref, wo_ref, g_ref, wrh_ref, wrl_ref, br_ref,
                      h_out_ref, xg_ref, counts_ref, cnt_ref):
    @pl.when(pl.program_id(0) == 0)
    def _():
        cnt_ref[...] = jnp.zeros_like(cnt_ref)

    h_new = h_ref[...] + jnp.dot(o_ref[...], wo_ref[...], preferred_element_type=_F32)
    h_out_ref[...] = h_new
    _route(h_new, g_ref, wrh_ref, wrl_ref, br_ref, cnt_ref, xg_ref, counts_ref)


def _const_spec(shape):
    return pl.BlockSpec(shape, lambda *_: (0,) * len(shape))


def _post_attn(o2, h2, w_o, gain, wrh, wrl, br):
    t = h2.shape[0]
    tm = _pick_tile(t, (512, 256))
    row = pl.BlockSpec((tm, D_MODEL), lambda i: (i, 0))
    return pl.pallas_call(
        _post_attn_kernel,
        grid=(t // tm,),
        in_specs=[row, row, _const_spec((D_MODEL, D_MODEL)), _const_spec((1, D_MODEL)),
                  _const_spec((D_MODEL, LANES)), _const_spec((D_MODEL, LANES)), _const_spec((1, LANES))],
        out_specs=[row, pl.BlockSpec((tm, XG_WIDTH), lambda i: (i, 0)), _const_spec((1, LANES))],
        out_shape=[jax.ShapeDtypeStruct((t, D_MODEL), _F32), jax.ShapeDtypeStruct((t, XG_WIDTH), _F32),
                   jax.ShapeDtypeStruct((1, LANES), _F32)],
        scratch_shapes=[pltpu.VMEM((1, LANES), _F32)],
        compiler_params=_params("arbitrary"),
        name="attn_out_route",
    )(o2, h2, w_o, gain, wrh, wrl, br)


def _pool_kernel(h_ref, halo_ref, mg_ref, pw_ref, ps_ref, g_ref, wrh_ref, wrl_ref, br_ref,
                 h_out_ref, xg_ref, counts_ref, cnt_ref, cat_ref, *, tile):
    i = pl.program_id(1)

    @pl.when((pl.program_id(0) == 0) & (i == 0))
    def _():
        cnt_ref[...] = jnp.zeros_like(cnt_ref)

    h = h_ref[0]
    gain = mg_ref[...]
    hn = _rms(h, gain)
    cat_ref[0:POOL_HALO, :] = jnp.where(i == 0, 0.0, _rms(halo_ref[0], gain))
    cat_ref[POOL_HALO:POOL_HALO + tile, :] = hn
    token = i * tile + lax.broadcasted_iota(jnp.int32, (tile, 1), 0) - FRONT
    outs = []
    for gi, w in enumerate(POOL_WINDOWS):
        cols = slice(gi * POOL_GROUP_DIM, (gi + 1) * POOL_GROUP_DIM)
        xg = hn[:, cols]
        wsum = xg
        for d in range(1, w):
            wsum = wsum + cat_ref[pl.ds(POOL_HALO - d, tile), cols]
        count = jnp.clip(token + 1, 1, w).astype(_F32)
        mixed = wsum / count - xg
        outs.append(jnp.dot(mixed.astype(_BF16), pw_ref[gi], preferred_element_type=_F32))
    h_new = h + jnp.concatenate(outs, axis=1) * ps_ref[...]
    h_out_ref[0] = h_new
    _route(h_new, g_ref, wrh_ref, wrl_ref, br_ref, cnt_ref, xg_ref, counts_ref)


def _pool(h3, mix_gain, pool_w, pool_scale, gain, wrh, wrl, br):
    b, p, _ = h3.shape
    tile = _pick_tile(p, (768, 256))
    nt = p // tile
    halo_blocks = tile // POOL_HALO
    row = pl.BlockSpec((1, tile, D_MODEL), lambda bi, i: (bi, i, 0))
    halo = pl.BlockSpec((1, POOL_HALO, D_MODEL), lambda bi, i: (bi, jnp.maximum(i * halo_blocks - 1, 0), 0))
    h_new, xg, counts = pl.pallas_call(
        functools.partial(_pool_kernel, tile=tile),
        grid=(b, nt),
        in_specs=[row, halo, _const_spec((1, D_MODEL)),
                  _const_spec((len(POOL_WINDOWS), POOL_GROUP_DIM, POOL_GROUP_DIM)), _const_spec((1, D_MODEL)),
                  _const_spec((1, D_MODEL)), _const_spec((D_MODEL, LANES)), _const_spec((D_MODEL, LANES)),
                  _const_spec((1, LANES))],
        out_specs=[row, pl.BlockSpec((tile, XG_WIDTH), lambda bi, i: (bi * nt + i, 0)), _const_spec((1, LANES))],
        out_shape=[jax.ShapeDtypeStruct(h3.shape, _F32), jax.ShapeDtypeStruct((b * p, XG_WIDTH), _F32),
                   jax.ShapeDtypeStruct((1, LANES), _F32)],
        scratch_shapes=[pltpu.VMEM((1, LANES), _F32), pltpu.VMEM((POOL_HALO + tile, D_MODEL), _F32)],
        compiler_params=_params("arbitrary", "arbitrary"),
        name="pool_route",
    )(h3, h3, mix_gain, pool_w, pool_scale, gain, wrh, wrl, br)
    return h_new, xg, counts


def _row_copy(src_ref, src_row, dst_ref, dst_row, sem):
    return pltpu.make_async_copy(src_ref.at[pl.ds(src_row, 1)], dst_ref.at[pl.ds(dst_row, 1)], sem)


def _dispatch_kernel(dest_ref, xg_ref, init_ref, out_ref, sem, *, tm):
    del init_ref

    def issue(r, c):
        _row_copy(xg_ref, r, out_ref, dest_ref[0, 0, r], sem).start()
        return c

    lax.fori_loop(0, tm, issue, 0, unroll=8)

    def wait(r, c):
        _row_copy(xg_ref, 0, out_ref, 0, sem).wait()
        return c

    lax.fori_loop(0, tm, wait, 0, unroll=8)


def _dispatch(xg, dest3, n_rows):
    t = xg.shape[0]
    tm = dest3.shape[-1]
    return pl.pallas_call(
        functools.partial(_dispatch_kernel, tm=tm),
        grid=(t // tm,),
        in_specs=[pl.BlockSpec((1, 1, tm), lambda i: (i, 0, 0), memory_space=pltpu.SMEM),
                  pl.BlockSpec((tm, XG_WIDTH), lambda i: (i, 0)),
                  pl.BlockSpec(memory_space=pl.ANY)],
        out_specs=pl.BlockSpec(memory_space=pl.ANY),
        out_shape=jax.ShapeDtypeStruct((n_rows, XG_WIDTH), _F32),
        scratch_shapes=[pltpu.SemaphoreType.DMA(())],
        input_output_aliases={2: 0},
        compiler_params=_params("arbitrary"),
        name="moe_dispatch",
    )(dest3, xg, jnp.zeros((n_rows, XG_WIDTH), _F32))


def _expert_kernel(lo_ref, hi_ref, used_ref, x_ref, wg_lo, wu_lo, wd_lo, wg_hi, wu_hi, wd_hi, y_ref):
    del lo_ref, hi_ref

    @pl.when(pl.program_id(0) < used_ref[0])
    def _():
        xb = x_ref[:, 0:D_MODEL].astype(_BF16)

        def expert(wg, wu, wd):
            gate = jnp.dot(xb, wg[0], preferred_element_type=_F32)
            up = jnp.dot(xb, wu[0], preferred_element_type=_F32)
            hid = gate * jax.nn.sigmoid(gate) * up
            return jnp.dot(hid.astype(_BF16), wd[0], preferred_element_type=_F32)

        g_lo = x_ref[:, D_MODEL + GATE_LO_LANE:D_MODEL + GATE_LO_LANE + 1]
        g_hi = x_ref[:, D_MODEL + GATE_HI_LANE:D_MODEL + GATE_HI_LANE + 1]
        y_ref[...] = expert(wg_lo, wu_lo, wd_lo) * g_lo + expert(wg_hi, wu_hi, wd_hi) * g_hi

    @pl.when(pl.program_id(0) >= used_ref[0])
    def _():
        y_ref[...] = jnp.zeros_like(y_ref)


def _experts(x_sorted, blk_lo, blk_hi, n_used, w_gate, w_up, w_down):
    n_rows = x_sorted.shape[0]
    nb = n_rows // ROW_BLOCK

    def blk(i, lo, hi, used):
        return jnp.minimum(i, used[0] - 1)

    def w_in(which):
        def index(i, lo, hi, used):
            table = lo if which == 0 else hi
            return (table[blk(i, lo, hi, used)], 0, 0)
        return pl.BlockSpec((1, D_MODEL, EXPERT_FF), index)

    def w_out(which):
        def index(i, lo, hi, used):
            table = lo if which == 0 else hi
            return (table[blk(i, lo, hi, used)], 0, 0)
        return pl.BlockSpec((1, EXPERT_FF, D_MODEL), index)

    grid_spec = pltpu.PrefetchScalarGridSpec(
        num_scalar_prefetch=3,
        grid=(nb,),
        in_specs=[pl.BlockSpec((ROW_BLOCK, XG_WIDTH), lambda i, lo, hi, used: (blk(i, lo, hi, used), 0)),
                  w_in(0), w_in(0), w_out(0), w_in(1), w_in(1), w_out(1)],
        out_specs=pl.BlockSpec((ROW_BLOCK, D_MODEL), lambda i, lo, hi, used: (i, 0)),
    )
    return pl.pallas_call(
        _expert_kernel,
        grid_spec=grid_spec,
        out_shape=jax.ShapeDtypeStruct((n_rows, D_MODEL), _F32),
        compiler_params=_params("arbitrary"),
        name="moe_experts",
    )(blk_lo, blk_hi, n_used, x_sorted, w_gate, w_up, w_down, w_gate, w_up, w_down)


def _combine_kernel(dest_ref, y_ref, h_ref, out_ref, rows_ref, sem, *, tm):
    def issue(r, c):
        _row_copy(y_ref, dest_ref[0, 0, r], rows_ref, r, sem).start()
        return c

    lax.fori_loop(0, tm, issue, 0, unroll=8)

    def wait(r, c):
        _row_copy(y_ref, 0, rows_ref, 0, sem).wait()
        return c

    lax.fori_loop(0, tm, wait, 0, unroll=8)
    out_ref[...] = h_ref[...] + rows_ref[...]


def _combine(y_sorted, dest3, h2):
    t = h2.shape[0]
    tm = dest3.shape[-1]
    row = pl.BlockSpec((tm, D_MODEL), lambda i: (i, 0))
    return pl.pallas_call(
        functools.partial(_combine_kernel, tm=tm),
        grid=(t // tm,),
        in_specs=[pl.BlockSpec((1, 1, tm), lambda i: (i, 0, 0), memory_space=pltpu.SMEM),
                  pl.BlockSpec(memory_space=pl.ANY), row],
        out_specs=row,
        out_shape=jax.ShapeDtypeStruct(h2.shape, _F32),
        scratch_shapes=[pltpu.VMEM((tm, D_MODEL), _F32), pltpu.SemaphoreType.DMA(())],
        input_output_aliases={2: 0},
        compiler_params=_params("arbitrary"),
        name="moe_combine",
    )(dest3, y_sorted, h2)


def _moe(h2, xg, counts, w_gate, w_up, w_down):
    t = h2.shape[0]
    tm = _pick_tile(t, (512, 256))
    n_blocks = (t + N_CLASSES * (ROW_BLOCK - 1)) // ROW_BLOCK
    cnt = counts[0, :N_CLASSES].astype(jnp.int32)
    padded = (cnt + ROW_BLOCK - 1) // ROW_BLOCK * ROW_BLOCK
    ends = jnp.cumsum(padded)
    starts = ends - padded
    n_used = (ends[-1:] // ROW_BLOCK).astype(jnp.int32)
    blk_cls = jnp.minimum(
        jnp.searchsorted(ends, jnp.arange(n_blocks, dtype=jnp.int32) * ROW_BLOCK, side="right"), N_CLASSES - 1)
    blk_lo = jnp.asarray(_CLASS_LO)[blk_cls]
    blk_hi = jnp.asarray(_CLASS_HI)[blk_cls]
    cls = xg[:, D_MODEL + CLASS_LANE].astype(jnp.int32)
    rank = xg[:, D_MODEL + RANK_LANE].astype(jnp.int32)
    dest3 = (starts[cls] + rank).astype(jnp.int32).reshape(t // tm, 1, tm)
    x_sorted = _dispatch(xg, dest3, n_blocks * ROW_BLOCK)
    y_sorted = _experts(x_sorted, blk_lo, blk_hi, n_used, w_gate, w_up, w_down)
    return _combine(y_sorted, dest3, h2)


def _final_kernel(h_ref, g_ref, o_ref):
    o_ref[0] = _rms(h_ref[0], g_ref[...])


def _final(h3, gain, seq):
    b = h3.shape[0]
    return pl.pallas_call(
        _final_kernel,
        grid=(b, seq // ATTN_TILE),
        in_specs=[pl.BlockSpec((1, ATTN_TILE, D_MODEL), lambda bi, i: (bi, i + 1, 0)), _const_spec((1, D_MODEL))],
        out_specs=pl.BlockSpec((1, ATTN_TILE, D_MODEL), lambda bi, i: (bi, i, 0)),
        out_shape=jax.ShapeDtypeStruct((b, seq, D_MODEL), _F32),
        compiler_params=_params("parallel", "parallel"),
        name="final_norm",
    )(h3, gain)


def kernel(x, meta_tokens, mix_norm, ffn_norm, final_norm, sb_w_qkv, sb_w_o, pool_w, pool_scale,
           router_group_w, router_group_b, router_expert_w, router_expert_b,
           expert_w_gate, expert_w_up, expert_w_down):
    b, seq, d = x.shape
    assert d == D_MODEL and seq % ATTN_TILE == 0
    depth = mix_norm.shape[0]
    p = seq + ATTN_TILE
    t = b * p
    meta = jnp.broadcast_to(meta_tokens[None].astype(x.dtype), (b, N_META, d))
    h3 = jnp.concatenate([jnp.zeros((b, FRONT, d), x.dtype), meta, x], axis=1)

    for i in range(depth):
        j = i // 2
        wrh, wrl, br = _router_weights(router_group_w[i], router_group_b[i], router_expert_w[i], router_expert_b[i])
        ffn_gain = ffn_norm[i].reshape(1, d)
        mix_gain = mix_norm[i].reshape(1, d)
        if i % 2 == 0:
            q, k, v = _qkv(h3.reshape(t, d), mix_gain, sb_w_qkv[j].astype(_BF16))
            o = _attention(q.reshape(b, p, d), k.reshape(b, p, d), v.reshape(b, p, d))
            h2, xg, counts = _post_attn(o.reshape(t, d), h3.reshape(t, d), sb_w_o[j].astype(_BF16),
                                        ffn_gain, wrh, wrl, br)
        else:
            h3, xg, counts = _pool(h3, mix_gain, pool_w[j].astype(_BF16), pool_scale[j].reshape(1, d),
                                   ffn_gain, wrh, wrl, br)
            h2 = h3.reshape(t, d)
        h2 = _moe(h2, xg, counts, expert_w_gate[i].astype(_BF16), expert_w_up[i].astype(_BF16),
                  expert_w_down[i].astype(_BF16))
        h3 = h2.reshape(b, p, d)
    return _final(h3, final_norm.reshape(1, d), seq)
```

```python
import functools

import numpy as np
import jax
import jax.numpy as jnp
from jax import lax
from jax.experimental import pallas as pl
from jax.experimental.pallas import tpu as pltpu

D_MODEL = 1024
N_META = 16
HEADS = 8
HEAD_DIM = D_MODEL // HEADS
POOL_WINDOWS = (2, 4, 8, 16)
POOL_GROUP_DIM = D_MODEL // len(POOL_WINDOWS)
POOL_HALO = 16
MOE_GROUPS = 4
EXPERTS_PER_GROUP = 8
N_EXPERTS = MOE_GROUPS * EXPERTS_PER_GROUP
EXPERT_FF = D_MODEL // 4
PAIRS_PER_GROUP = EXPERTS_PER_GROUP * (EXPERTS_PER_GROUP - 1) // 2
N_CLASSES = MOE_GROUPS * PAIRS_PER_GROUP
RMS_EPS = 1e-6

LANES = 128
ATTN_TILE = 256
FRONT = ATTN_TILE - N_META
HEADS_PER_STEP = 2
LOG2_E = 1.4426950408889634
EXHAUSTED_LOG2 = -200.0
ROW_BLOCK = 256
XG_WIDTH = D_MODEL + LANES
GATE_LO_LANE, GATE_HI_LANE, CLASS_LANE, RANK_LANE = 0, 1, 2, 3
NEG_BIG = -1e30
VMEM_LIMIT = 48 * 1024 * 1024

_F32 = jnp.float32
_BF16 = jnp.bfloat16


def _class_tables():
    lo, hi = [], []
    for g in range(MOE_GROUPS):
        for a in range(EXPERTS_PER_GROUP):
            for b in range(a + 1, EXPERTS_PER_GROUP):
                lo.append(g * EXPERTS_PER_GROUP + a)
                hi.append(g * EXPERTS_PER_GROUP + b)
    return np.asarray(lo, np.int32), np.asarray(hi, np.int32)


_CLASS_LO, _CLASS_HI = _class_tables()


def _pick_tile(n, candidates):
    for c in candidates:
        if n % c == 0:
            return c
    raise ValueError(f"no tile in {candidates} divides {n}")


def _params(*semantics):
    return pltpu.CompilerParams(dimension_semantics=semantics, vmem_limit_bytes=VMEM_LIMIT)


def _rms(x, g):
    return x * lax.rsqrt(jnp.mean(x * x, axis=-1, keepdims=True) + RMS_EPS) * g


def _qkv_kernel(h_ref, g_ref, w_ref, q_ref, k_ref, v_ref):
    xn = _rms(h_ref[...], g_ref[...]).astype(_BF16)
    scale = HEAD_DIM ** -0.5 * LOG2_E
    q = jnp.dot(xn, w_ref[:, 0:D_MODEL], preferred_element_type=_F32)
    q_ref[...] = (q * scale).astype(q_ref.dtype)
    k = jnp.dot(xn, w_ref[:, D_MODEL:2 * D_MODEL], preferred_element_type=_F32)
    k_ref[...] = k.astype(k_ref.dtype)
    v = jnp.dot(xn, w_ref[:, 2 * D_MODEL:3 * D_MODEL], preferred_element_type=_F32)
    v_ref[...] = v.astype(v_ref.dtype)


def _qkv(h2, gain, w_qkv):
    t = h2.shape[0]
    tm = _pick_tile(t, (512, 256))
    row = pl.BlockSpec((tm, D_MODEL), lambda i: (i, 0))
    out = jax.ShapeDtypeStruct((t, D_MODEL), _BF16)
    return pl.pallas_call(
        _qkv_kernel,
        grid=(t // tm,),
        in_specs=[row, pl.BlockSpec((1, D_MODEL), lambda i: (0, 0)),
                  pl.BlockSpec((D_MODEL, 3 * D_MODEL), lambda i: (0, 0))],
        out_specs=[row, row, row],
        out_shape=[out, out, out],
        compiler_params=_params("parallel"),
        name="qkv_proj",
    )(h2, gain, w_qkv)


def _neg_abs(x):
    bits = lax.bitcast_convert_type(x, jnp.uint32) | jnp.uint32(0x80000000)
    return lax.bitcast_convert_type(bits, _F32)


def _attn_kernel(q_ref, k_ref, v_ref, o_ref, acc_ref, carry_ref):
    qi = pl.program_id(2)
    tri_r = lax.broadcasted_iota(jnp.int32, (ATTN_TILE, ATTN_TILE), 0)
    tri_c = lax.broadcasted_iota(jnp.int32, (ATTN_TILE, ATTN_TILE), 1)
    neg_suffix = jnp.where(tri_r > tri_c, -1.0, 0.0).astype(_BF16)
    acc_ref[...] = jnp.zeros_like(acc_ref)
    carry_ref[...] = jnp.zeros_like(carry_ref)

    def step(kb, masked):
        start = pl.multiple_of(kb * ATTN_TILE, ATTN_TILE)
        if masked:
            q_pos = qi * ATTN_TILE + tri_r
            k_pos = kb * ATTN_TILE + tri_c
            mask = (k_pos < q_pos) & (k_pos >= FRONT)
        for g in range(HEADS_PER_STEP):
            cols = slice(g * HEAD_DIM, (g + 1) * HEAD_DIM)
            k = k_ref[0, pl.ds(start, ATTN_TILE), cols]
            v = v_ref[0, pl.ds(start, ATTN_TILE), cols]
            z = lax.dot_general(q_ref[0, :, cols], k, (((1,), (1,)), ((), ())), preferred_element_type=_F32)
            sp = jnp.maximum(z, 0.0) + jnp.log(1.0 + jnp.exp2(_neg_abs(z))) * LOG2_E
            if masked:
                sp = jnp.where(mask, sp, 0.0)
            within = jnp.dot(sp.astype(_BF16), neg_suffix, preferred_element_type=_F32)
            later = within + carry_ref[g]
            a = jnp.exp2(z - sp + later)
            if masked:
                a = jnp.where(mask, a, 0.0)
            acc_ref[g] += jnp.dot(a.astype(_BF16), v, preferred_element_type=_F32)
            carry_ref[g] = later[:, 0:1] - sp[:, 0:1]

    def alive():
        top = carry_ref[0]
        for g in range(1, HEADS_PER_STEP):
            top = jnp.maximum(top, carry_ref[g])
        return (jnp.max(top) > EXHAUSTED_LOG2).astype(jnp.int32)

    step(qi, True)

    def cond(state):
        kb, live = state
        return jnp.logical_and(kb >= 1, live > 0)

    def body(state):
        kb, _ = state
        step(kb, False)
        return kb - 1, alive()

    kb_end, live = lax.while_loop(cond, body, (qi - 1, alive()))

    @pl.when(jnp.logical_and(kb_end == 0, live > 0))
    def _():
        step(0, True)

    for g in range(HEADS_PER_STEP):
        o_ref[0, :, g * HEAD_DIM:(g + 1) * HEAD_DIM] = acc_ref[g].astype(o_ref.dtype)


def _attention(q, k, v):
    b, p, _ = q.shape
    nq = p // ATTN_TILE
    width = HEADS_PER_STEP * HEAD_DIM
    qspec = pl.BlockSpec((1, ATTN_TILE, width), lambda bi, hi, qi: (bi, qi, hi))
    kvspec = pl.BlockSpec((1, p, width), lambda bi, hi, qi: (bi, 0, hi))
    return pl.pallas_call(
        _attn_kernel,
        grid=(b, HEADS // HEADS_PER_STEP, nq),
        in_specs=[qspec, kvspec, kvspec],
        out_specs=qspec,
        out_shape=jax.ShapeDtypeStruct(q.shape, _BF16),
        scratch_shapes=[pltpu.VMEM((HEADS_PER_STEP, ATTN_TILE, HEAD_DIM), _F32),
                        pltpu.VMEM((HEADS_PER_STEP, ATTN_TILE, 1), _F32)],
        compiler_params=_params("parallel", "parallel", "arbitrary"),
        name="sb_attention",
    )(q, k, v)


def _route(h_new, g_ref, wrh_ref, wrl_ref, br_ref, cnt_ref, xg_ref, counts_ref):
    tm = h_new.shape[0]
    xn = _rms(h_new, g_ref[...])
    x_hi = xn.astype(_BF16)
    x_lo = (xn - x_hi.astype(_F32)).astype(_BF16)
    logits = (jnp.dot(x_hi, wrh_ref[...], preferred_element_type=_F32)
              + jnp.dot(x_lo, wrh_ref[...], preferred_element_type=_F32)
              + jnp.dot(x_hi, wrl_ref[...], preferred_element_type=_F32)) + br_ref[...]
    lane = lax.broadcasted_iota(jnp.int32, (tm, LANES), 1)

    def first_argmax(vals, vmax):
        return jnp.min(jnp.where(vals >= vmax, lane, LANES), axis=-1, keepdims=True)

    gl = jnp.where(lane < MOE_GROUPS, logits, NEG_BIG)
    gmax = jnp.max(gl, axis=-1, keepdims=True)
    gsel = first_argmax(gl, gmax)
    gprob = 1.0 / jnp.sum(jnp.exp(gl - gmax), axis=-1, keepdims=True)
    first = MOE_GROUPS + EXPERTS_PER_GROUP * gsel
    el = jnp.where((lane >= first) & (lane < first + EXPERTS_PER_GROUP), logits, NEG_BIG)
    m1 = jnp.max(el, axis=-1, keepdims=True)
    i1 = first_argmax(el, m1)
    el2 = jnp.where(lane == i1, NEG_BIG, el)
    m2 = jnp.max(el2, axis=-1, keepdims=True)
    i2 = first_argmax(el2, m2)
    ratio = jnp.exp(m2 - m1)
    gate1 = gprob / (1.0 + ratio)
    gate2 = gprob * ratio / (1.0 + ratio)
    a = jnp.minimum(i1, i2) - first
    bb = jnp.maximum(i1, i2) - first
    pair = (EXPERTS_PER_GROUP - 1) * a - jnp.right_shift(a * (a - 1), 1) + (bb - a - 1)
    cls = gsel * PAIRS_PER_GROUP + pair
    gate_lo = jnp.where(i1 < i2, gate1, gate2)
    gate_hi = jnp.where(i1 < i2, gate2, gate1)

    onehot = (lane == cls).astype(_F32)
    tri_r = lax.broadcasted_iota(jnp.int32, (tm, tm), 0)
    tri_c = lax.broadcasted_iota(jnp.int32, (tm, tm), 1)
    before = (tri_c < tri_r).astype(_BF16)
    seen = jnp.dot(before, onehot.astype(_BF16), preferred_element_type=_F32) + cnt_ref[...]
    rank = jnp.sum(onehot * seen, axis=-1, keepdims=True)
    cnt_ref[...] += jnp.sum(onehot, axis=0, keepdims=True)
    counts_ref[...] = cnt_ref[...]

    extra = jnp.where(lane == GATE_LO_LANE, gate_lo,
                      jnp.where(lane == GATE_HI_LANE, gate_hi,
                                jnp.where(lane == CLASS_LANE, cls.astype(_F32),
                                          jnp.where(lane == RANK_LANE, rank, 0.0))))
    xg_ref[:, 0:D_MODEL] = xn
    xg_ref[:, D_MODEL:XG_WIDTH] = extra


def _router_weights(wg, bg, we, be):
    w = jnp.zeros((D_MODEL, LANES), _F32)
    w = w.at[:, 0:MOE_GROUPS].set(wg).at[:, MOE_GROUPS:MOE_GROUPS + N_EXPERTS].set(we)
    bias = jnp.zeros((1, LANES), _F32)
    bias = bias.at[0, 0:MOE_GROUPS].set(bg).at[0, MOE_GROUPS:MOE_GROUPS + N_EXPERTS].set(be)
    w_hi = w.astype(_BF16)
    w_lo = (w - w_hi.astype(_F32)).astype(_BF16)
    return w_hi, w_lo, bias


def _post_attn_kernel(o_ref, h_ref, wo_ref, g_ref, wrh_ref, wrl_ref, br_ref,
                      h_out_ref, xg_ref, counts_ref, cnt_ref):
    @pl.when(pl.program_id(0) == 0)
    def _():
        cnt_ref[...] = jnp.zeros_like(cnt_ref)

    h_new = h_ref[...] + jnp.dot(o_ref[...], wo_ref[...], preferred_element_type=_F32)
    h_out_ref[...] = h_new
    _route(h_new, g_ref, wrh_ref, wrl_ref, br_ref, cnt_ref, xg_ref, counts_ref)


def _const_spec(shape):
    return pl.BlockSpec(shape, lambda *_: (0,) * len(shape))


def _post_attn(o2, h2, w_o, gain, wrh, wrl, br):
    t = h2.shape[0]
    tm = _pick_tile(t, (512, 256))
    row = pl.BlockSpec((tm, D_MODEL), lambda i: (i, 0))
    return pl.pallas_call(
        _post_attn_kernel,
        grid=(t // tm,),
        in_specs=[row, row, _const_spec((D_MODEL, D_MODEL)), _const_spec((1, D_MODEL)),
                  _const_spec((D_MODEL, LANES)), _const_spec((D_MODEL, LANES)), _const_spec((1, LANES))],
        out_specs=[row, pl.BlockSpec((tm, XG_WIDTH), lambda i: (i, 0)), _const_spec((1, LANES))],
        out_shape=[jax.ShapeDtypeStruct((t, D_MODEL), _F32), jax.ShapeDtypeStruct((t, XG_WIDTH), _F32),
                   jax.ShapeDtypeStruct((1, LANES), _F32)],
        scratch_shapes=[pltpu.VMEM((1, LANES), _F32)],
        compiler_params=_params("arbitrary"),
        name="attn_out_route",
    )(o2, h2, w_o, gain, wrh, wrl, br)


def _pool_kernel(h_ref, halo_ref, mg_ref, pw_ref, ps_ref, g_ref, wrh_ref, wrl_ref, br_ref,
                 h_out_ref, xg_ref, counts_ref, cnt_ref, cat_ref, *, tile):
    i = pl.program_id(1)

    @pl.when((pl.program_id(0) == 0) & (i == 0))
    def _():
        cnt_ref[...] = jnp.zeros_like(cnt_ref)

    h = h_ref[0]
    gain = mg_ref[...]
    hn = _rms(h, gain)
    cat_ref[0:POOL_HALO, :] = jnp.where(i == 0, 0.0, _rms(halo_ref[0], gain))
    cat_ref[POOL_HALO:POOL_HALO + tile, :] = hn
    token = i * tile + lax.broadcasted_iota(jnp.int32, (tile, 1), 0) - FRONT
    outs = []
    for gi, w in enumerate(POOL_WINDOWS):
        cols = slice(gi * POOL_GROUP_DIM, (gi + 1) * POOL_GROUP_DIM)
        xg = hn[:, cols]
        wsum = xg
        for d in range(1, w):
            wsum = wsum + cat_ref[pl.ds(POOL_HALO - d, tile), cols]
        count = jnp.clip(token + 1, 1, w).astype(_F32)
        mixed = wsum / count - xg
        outs.append(jnp.dot(mixed.astype(_BF16), pw_ref[gi], preferred_element_type=_F32))
    h_new = h + jnp.concatenate(outs, axis=1) * ps_ref[...]
    h_out_ref[0] = h_new
    _route(h_new, g_ref, wrh_ref, wrl_ref, br_ref, cnt_ref, xg_ref, counts_ref)


def _pool(h3, mix_gain, pool_w, pool_scale, gain, wrh, wrl, br):
    b, p, _ = h3.shape
    tile = _pick_tile(p, (768, 256))
    nt = p // tile
    halo_blocks = tile // POOL_HALO
    row = pl.BlockSpec((1, tile, D_MODEL), lambda bi, i: (bi, i, 0))
    halo = pl.BlockSpec((1, POOL_HALO, D_MODEL), lambda bi, i: (bi, jnp.maximum(i * halo_blocks - 1, 0), 0))
    h_new, xg, counts = pl.pallas_call(
        functools.partial(_pool_kernel, tile=tile),
        grid=(b, nt),
        in_specs=[row, halo, _const_spec((1, D_MODEL)),
                  _const_spec((len(POOL_WINDOWS), POOL_GROUP_DIM, POOL_GROUP_DIM)), _const_spec((1, D_MODEL)),
                  _const_spec((1, D_MODEL)), _const_spec((D_MODEL, LANES)), _const_spec((D_MODEL, LANES)),
                  _const_spec((1, LANES))],
        out_specs=[row, pl.BlockSpec((tile, XG_WIDTH), lambda bi, i: (bi * nt + i, 0)), _const_spec((1, LANES))],
        out_shape=[jax.ShapeDtypeStruct(h3.shape, _F32), jax.ShapeDtypeStruct((b * p, XG_WIDTH), _F32),
                   jax.ShapeDtypeStruct((1, LANES), _F32)],
        scratch_shapes=[pltpu.VMEM((1, LANES), _F32), pltpu.VMEM((POOL_HALO + tile, D_MODEL), _F32)],
        compiler_params=_params("arbitrary", "arbitrary"),
        name="pool_route",
    )(h3, h3, mix_gain, pool_w, pool_scale, gain, wrh, wrl, br)
    return h_new, xg, counts


def _row_copy(src_ref, src_row, dst_ref, dst_row, sem):
    return pltpu.make_async_copy(src_ref.at[pl.ds(src_row, 1)], dst_ref.at[pl.ds(dst_row, 1)], sem)


def _dispatch_kernel(dest_ref, xg_ref, init_ref, out_ref, sem, *, tm):
    del init_ref

    def issue(r, c):
        _row_copy(xg_ref, r, out_ref, dest_ref[0, 0, r], sem).start()
        return c

    lax.fori_loop(0, tm, issue, 0, unroll=8)

    def wait(r, c):
        _row_copy(xg_ref, 0, out_ref, 0, sem).wait()
        return c

    lax.fori_loop(0, tm, wait, 0, unroll=8)


def _dispatch(xg, dest3, n_rows):
    t = xg.shape[0]
    tm = dest3.shape[-1]
    return pl.pallas_call(
        functools.partial(_dispatch_kernel, tm=tm),
        grid=(t // tm,),
        in_specs=[pl.BlockSpec((1, 1, tm), lambda i: (i, 0, 0), memory_space=pltpu.SMEM),
                  pl.BlockSpec((tm, XG_WIDTH), lambda i: (i, 0)),
                  pl.BlockSpec(memory_space=pl.ANY)],
        out_specs=pl.BlockSpec(memory_space=pl.ANY),
        out_shape=jax.ShapeDtypeStruct((n_rows, XG_WIDTH), _F32),
        scratch_shapes=[pltpu.SemaphoreType.DMA(())],
        input_output_aliases={2: 0},
        compiler_params=_params("arbitrary"),
        name="moe_dispatch",
    )(dest3, xg, jnp.zeros((n_rows, XG_WIDTH), _F32))


def _expert_kernel(lo_ref, hi_ref, used_ref, x_ref, wg_lo, wu_lo, wd_lo, wg_hi, wu_hi, wd_hi, y_ref):
    del lo_ref, hi_ref

    @pl.when(pl.program_id(0) < used_ref[0])
    def _():
        xb = x_ref[:, 0:D_MODEL].astype(_BF16)

        def expert(wg, wu, wd):
            gate = jnp.dot(xb, wg[0], preferred_element_type=_F32)
            up = jnp.dot(xb, wu[0], preferred_element_type=_F32)
            hid = gate * jax.nn.sigmoid(gate) * up
            return jnp.dot(hid.astype(_BF16), wd[0], preferred_element_type=_F32)

        g_lo = x_ref[:, D_MODEL + GATE_LO_LANE:D_MODEL + GATE_LO_LANE + 1]
        g_hi = x_ref[:, D_MODEL + GATE_HI_LANE:D_MODEL + GATE_HI_LANE + 1]
        y_ref[...] = expert(wg_lo, wu_lo, wd_lo) * g_lo + expert(wg_hi, wu_hi, wd_hi) * g_hi

    @pl.when(pl.program_id(0) >= used_ref[0])
    def _():
        y_ref[...] = jnp.zeros_like(y_ref)


def _experts(x_sorted, blk_lo, blk_hi, n_used, w_gate, w_up, w_down):
    n_rows = x_sorted.shape[0]
    nb = n_rows // ROW_BLOCK

    def blk(i, lo, hi, used):
        return jnp.minimum(i, used[0] - 1)

    def w_in(which):
        def index(i, lo, hi, used):
            table = lo if which == 0 else hi
            return (table[blk(i, lo, hi, used)], 0, 0)
        return pl.BlockSpec((1, D_MODEL, EXPERT_FF), index)

    def w_out(which):
        def index(i, lo, hi, used):
            table = lo if which == 0 else hi
            return (table[blk(i, lo, hi, used)], 0, 0)
        return pl.BlockSpec((1, EXPERT_FF, D_MODEL), index)

    grid_spec = pltpu.PrefetchScalarGridSpec(
        num_scalar_prefetch=3,
        grid=(nb,),
        in_specs=[pl.BlockSpec((ROW_BLOCK, XG_WIDTH), lambda i, lo, hi, used: (blk(i, lo, hi, used), 0)),
                  w_in(0), w_in(0), w_out(0), w_in(1), w_in(1), w_out(1)],
        out_specs=pl.BlockSpec((ROW_BLOCK, D_MODEL), lambda i, lo, hi, used: (i, 0)),
    )
    return pl.pallas_call(
        _expert_kernel,
        grid_spec=grid_spec,
        out_shape=jax.ShapeDtypeStruct((n_rows, D_MODEL), _F32),
        compiler_params=_params("arbitrary"),
        name="moe_experts",
    )(blk_lo, blk_hi, n_used, x_sorted, w_gate, w_up, w_down, w_gate, w_up, w_down)


def _combine_kernel(dest_ref, y_ref, h_ref, out_ref, rows_ref, sem, *, tm):
    def issue(r, c):
        _row_copy(y_ref, dest_ref[0, 0, r], rows_ref, r, sem).start()
        return c

    lax.fori_loop(0, tm, issue, 0, unroll=8)

    def wait(r, c):
        _row_copy(y_ref, 0, rows_ref, 0, sem).wait()
        return c

    lax.fori_loop(0, tm, wait, 0, unroll=8)
    out_ref[...] = h_ref[...] + rows_ref[...]


def _combine(y_sorted, dest3, h2):
    t = h2.shape[0]
    tm = dest3.shape[-1]
    row = pl.BlockSpec((tm, D_MODEL), lambda i: (i, 0))
    return pl.pallas_call(
        functools.partial(_combine_kernel, tm=tm),
        grid=(t // tm,),
        in_specs=[pl.BlockSpec((1, 1, tm), lambda i: (i, 0, 0), memory_space=pltpu.SMEM),
                  pl.BlockSpec(memory_space=pl.ANY), row],
        out_specs=row,
        out_shape=jax.ShapeDtypeStruct(h2.shape, _F32),
        scratch_shapes=[pltpu.VMEM((tm, D_MODEL), _F32), pltpu.SemaphoreType.DMA(())],
        input_output_aliases={2: 0},
        compiler_params=_params("arbitrary"),
        name="moe_combine",
    )(dest3, y_sorted, h2)


def _moe(h2, xg, counts, w_gate, w_up, w_down):
    t = h2.shape[0]
    tm = _pick_tile(t, (512, 256))
    n_blocks = (t + N_CLASSES * (ROW_BLOCK - 1)) // ROW_BLOCK
    cnt = counts[0, :N_CLASSES].astype(jnp.int32)
    padded = (cnt + ROW_BLOCK - 1) // ROW_BLOCK * ROW_BLOCK
    ends = jnp.cumsum(padded)
    starts = ends - padded
    n_used = (ends[-1:] // ROW_BLOCK).astype(jnp.int32)
    blk_cls = jnp.minimum(
        jnp.searchsorted(ends, jnp.arange(n_blocks, dtype=jnp.int32) * ROW_BLOCK, side="right"), N_CLASSES - 1)
    blk_lo = jnp.asarray(_CLASS_LO)[blk_cls]
    blk_hi = jnp.asarray(_CLASS_HI)[blk_cls]
    cls = xg[:, D_MODEL + CLASS_LANE].astype(jnp.int32)
    rank = xg[:, D_MODEL + RANK_LANE].astype(jnp.int32)
    class_ids = jnp.arange(N_CLASSES, dtype=jnp.int32)
    start_of = jnp.sum(jnp.where(cls[:, None] == class_ids[None, :], starts[None, :], 0), axis=1)
    dest3 = (start_of + rank).astype(jnp.int32).reshape(t // tm, 1, tm)
    x_sorted = _dispatch(xg, dest3, n_blocks * ROW_BLOCK)
    y_sorted = _experts(x_sorted, blk_lo, blk_hi, n_used, w_gate, w_up, w_down)
    return _combine(y_sorted, dest3, h2)


def _final_kernel(h_ref, g_ref, o_ref):
    o_ref[0] = _rms(h_ref[0], g_ref[...])


def _final(h3, gain, seq):
    b = h3.shape[0]
    return pl.pallas_call(
        _final_kernel,
        grid=(b, seq // ATTN_TILE),
        in_specs=[pl.BlockSpec((1, ATTN_TILE, D_MODEL), lambda bi, i: (bi, i + 1, 0)), _const_spec((1, D_MODEL))],
        out_specs=pl.BlockSpec((1, ATTN_TILE, D_MODEL), lambda bi, i: (bi, i, 0)),
        out_shape=jax.ShapeDtypeStruct((b, seq, D_MODEL), _F32),
        compiler_params=_params("parallel", "parallel"),
        name="final_norm",
    )(h3, gain)


def kernel(x, meta_tokens, mix_norm, ffn_norm, final_norm, sb_w_qkv, sb_w_o, pool_w, pool_scale,
           router_group_w, router_group_b, router_expert_w, router_expert_b,
           expert_w_gate, expert_w_up, expert_w_down):
    b, seq, d = x.shape
    assert d == D_MODEL and seq % ATTN_TILE == 0
    depth = mix_norm.shape[0]
    p = seq + ATTN_TILE
    t = b * p
    meta = jnp.broadcast_to(meta_tokens[None].astype(x.dtype), (b, N_META, d))
    h3 = jnp.concatenate([jnp.zeros((b, FRONT, d), x.dtype), meta, x], axis=1)

    for i in range(depth):
        j = i // 2
        wrh, wrl, br = _router_weights(router_group_w[i], router_group_b[i], router_expert_w[i], router_expert_b[i])
        ffn_gain = ffn_norm[i].reshape(1, d)
        mix_gain = mix_norm[i].reshape(1, d)
        if i % 2 == 0:
            q, k, v = _qkv(h3.reshape(t, d), mix_gain, sb_w_qkv[j].astype(_BF16))
            o = _attention(q.reshape(b, p, d), k.reshape(b, p, d), v.reshape(b, p, d))
            h2, xg, counts = _post_attn(o.reshape(t, d), h3.reshape(t, d), sb_w_o[j].astype(_BF16),
                                        ffn_gain, wrh, wrl, br)
        else:
            h3, xg, counts = _pool(h3, mix_gain, pool_w[j].astype(_BF16), pool_scale[j].reshape(1, d),
                                   ffn_gain, wrh, wrl, br)
            h2 = h3.reshape(t, d)
        h2 = _moe(h2, xg, counts, expert_w_gate[i].astype(_BF16), expert_w_up[i].astype(_BF16),
                  expert_w_down[i].astype(_BF16))
        h3 = h2.reshape(b, p, d)
    return _final(h3, final_norm.reshape(1, d), seq)
```

```python
import functools

import numpy as np
import jax
import jax.numpy as jnp
from jax import lax
from jax.experimental import pallas as pl
from jax.experimental.pallas import tpu as pltpu

D_MODEL = 1024
N_META = 16
HEADS = 8
HEAD_DIM = D_MODEL // HEADS
POOL_WINDOWS = (2, 4, 8, 16)
POOL_GROUP_DIM = D_MODEL // len(POOL_WINDOWS)
POOL_HALO = 16
MOE_GROUPS = 4
EXPERTS_PER_GROUP = 8
N_EXPERTS = MOE_GROUPS * EXPERTS_PER_GROUP
EXPERT_FF = D_MODEL // 4
PAIRS_PER_GROUP = EXPERTS_PER_GROUP * (EXPERTS_PER_GROUP - 1) // 2
N_CLASSES = MOE_GROUPS * PAIRS_PER_GROUP
RMS_EPS = 1e-6

LANES = 128
SUBLANES = 8
ATTN_TILE = 256
FRONT = ATTN_TILE - N_META
HEADS_PER_STEP = 4
LOG2_E = 1.4426950408889634
EXHAUSTED_LOG2 = -200.0
ROW_BLOCK = 256
XG_WIDTH = D_MODEL + LANES
GATE_LO_LANE, GATE_HI_LANE, CLASS_LANE, RANK_LANE = 0, 1, 2, 3
NEG_BIG = -1e30
VMEM_LIMIT = 48 * 1024 * 1024

_F32 = jnp.float32
_BF16 = jnp.bfloat16


def _class_tables():
    lo, hi = [], []
    for g in range(MOE_GROUPS):
        for a in range(EXPERTS_PER_GROUP):
            for b in range(a + 1, EXPERTS_PER_GROUP):
                lo.append(g * EXPERTS_PER_GROUP + a)
                hi.append(g * EXPERTS_PER_GROUP + b)
    return np.asarray(lo, np.int32), np.asarray(hi, np.int32)


_CLASS_LO, _CLASS_HI = _class_tables()


def _pick_tile(n, candidates):
    for c in candidates:
        if n % c == 0:
            return c
    raise ValueError(f"no tile in {candidates} divides {n}")


def _params(*semantics):
    return pltpu.CompilerParams(dimension_semantics=semantics, vmem_limit_bytes=VMEM_LIMIT)


def _rms(x, g):
    return x * lax.rsqrt(jnp.mean(x * x, axis=-1, keepdims=True) + RMS_EPS) * g


def _qkv_kernel(h_ref, g_ref, w_ref, q_ref, k_ref, v_ref):
    xn = _rms(h_ref[...], g_ref[...]).astype(_BF16)
    scale = HEAD_DIM ** -0.5 * LOG2_E
    q = jnp.dot(xn, w_ref[:, 0:D_MODEL], preferred_element_type=_F32)
    q_ref[...] = (q * scale).astype(q_ref.dtype)
    k = jnp.dot(xn, w_ref[:, D_MODEL:2 * D_MODEL], preferred_element_type=_F32)
    k_ref[...] = k.astype(k_ref.dtype)
    v = jnp.dot(xn, w_ref[:, 2 * D_MODEL:3 * D_MODEL], preferred_element_type=_F32)
    v_ref[...] = v.astype(v_ref.dtype)


def _qkv(h2, gain, w_qkv):
    t = h2.shape[0]
    tm = _pick_tile(t, (512, 256))
    row = pl.BlockSpec((tm, D_MODEL), lambda i: (i, 0))
    out = jax.ShapeDtypeStruct((t, D_MODEL), _BF16)
    return pl.pallas_call(
        _qkv_kernel,
        grid=(t // tm,),
        in_specs=[row, pl.BlockSpec((1, D_MODEL), lambda i: (0, 0)),
                  pl.BlockSpec((D_MODEL, 3 * D_MODEL), lambda i: (0, 0))],
        out_specs=[row, row, row],
        out_shape=[out, out, out],
        compiler_params=_params("parallel"),
        name="qkv_proj",
    )(h2, gain, w_qkv)


def _neg_abs(x):
    bits = lax.bitcast_convert_type(x, jnp.uint32) | jnp.uint32(0x80000000)
    return lax.bitcast_convert_type(bits, _F32)


def _attn_kernel(q_ref, k_ref, v_ref, o_ref, acc_ref, carry_ref):
    qi = pl.program_id(2)
    tri_r = lax.broadcasted_iota(jnp.int32, (ATTN_TILE, ATTN_TILE), 0)
    tri_c = lax.broadcasted_iota(jnp.int32, (ATTN_TILE, ATTN_TILE), 1)
    neg_suffix = jnp.where(tri_r > tri_c, -1.0, 0.0).astype(_BF16)
    acc_ref[...] = jnp.zeros_like(acc_ref)
    carry_ref[...] = jnp.zeros_like(carry_ref)

    def step(kb, masked):
        start = pl.multiple_of(kb * ATTN_TILE, ATTN_TILE)
        if masked:
            q_pos = qi * ATTN_TILE + tri_r
            k_pos = kb * ATTN_TILE + tri_c
            mask = (k_pos < q_pos) & (k_pos >= FRONT)
        for g in range(HEADS_PER_STEP):
            cols = slice(g * HEAD_DIM, (g + 1) * HEAD_DIM)
            k = k_ref[0, pl.ds(start, ATTN_TILE), cols]
            v = v_ref[0, pl.ds(start, ATTN_TILE), cols]
            z = lax.dot_general(q_ref[0, :, cols], k, (((1,), (1,)), ((), ())), preferred_element_type=_F32)
            sp = jnp.maximum(z, 0.0) + jnp.log(1.0 + jnp.exp2(_neg_abs(z))) * LOG2_E
            if masked:
                sp = jnp.where(mask, sp, 0.0)
            within = jnp.dot(sp.astype(_BF16), neg_suffix, preferred_element_type=_F32)
            later = within + carry_ref[g]
            a = jnp.exp2(z - sp + later)
            if masked:
                a = jnp.where(mask, a, 0.0)
            acc_ref[g] += jnp.dot(a.astype(_BF16), v, preferred_element_type=_F32)
            carry_ref[g] = later[:, 0:1] - sp[:, 0:1]

    def alive():
        top = carry_ref[0]
        for g in range(1, HEADS_PER_STEP):
            top = jnp.maximum(top, carry_ref[g])
        return (jnp.max(top) > EXHAUSTED_LOG2).astype(jnp.int32)

    step(qi, True)

    def cond(state):
        kb, live = state
        return jnp.logical_and(kb >= 1, live > 0)

    def body(state):
        kb, _ = state
        step(kb, False)
        return kb - 1, alive()

    kb_end, live = lax.while_loop(cond, body, (qi - 1, alive()))

    @pl.when(jnp.logical_and(kb_end == 0, live > 0))
    def _():
        step(0, True)

    for g in range(HEADS_PER_STEP):
        o_ref[0, :, g * HEAD_DIM:(g + 1) * HEAD_DIM] = acc_ref[g].astype(o_ref.dtype)


def _attention(q, k, v):
    b, p, _ = q.shape
    nq = p // ATTN_TILE
    width = HEADS_PER_STEP * HEAD_DIM
    qspec = pl.BlockSpec((1, ATTN_TILE, width), lambda bi, hi, qi: (bi, qi, hi))
    kvspec = pl.BlockSpec((1, p, width), lambda bi, hi, qi: (bi, 0, hi))
    return pl.pallas_call(
        _attn_kernel,
        grid=(b, HEADS // HEADS_PER_STEP, nq),
        in_specs=[qspec, kvspec, kvspec],
        out_specs=qspec,
        out_shape=jax.ShapeDtypeStruct(q.shape, _BF16),
        scratch_shapes=[pltpu.VMEM((HEADS_PER_STEP, ATTN_TILE, HEAD_DIM), _F32),
                        pltpu.VMEM((HEADS_PER_STEP, ATTN_TILE, 1), _F32)],
        compiler_params=_params("parallel", "parallel", "arbitrary"),
        name="sb_attention",
    )(q, k, v)


def _route(h_new, g_ref, wrh_ref, wrl_ref, br_ref, cnt_ref, xg_ref, counts_ref):
    tm = h_new.shape[0]
    xn = _rms(h_new, g_ref[...])
    x_hi = xn.astype(_BF16)
    x_lo = (xn - x_hi.astype(_F32)).astype(_BF16)
    logits = (jnp.dot(x_hi, wrh_ref[...], preferred_element_type=_F32)
              + jnp.dot(x_lo, wrh_ref[...], preferred_element_type=_F32)
              + jnp.dot(x_hi, wrl_ref[...], preferred_element_type=_F32)) + br_ref[...]
    lane = lax.broadcasted_iota(jnp.int32, (tm, LANES), 1)

    def first_argmax(vals, vmax):
        return jnp.min(jnp.where(vals >= vmax, lane, LANES), axis=-1, keepdims=True)

    gl = jnp.where(lane < MOE_GROUPS, logits, NEG_BIG)
    gmax = jnp.max(gl, axis=-1, keepdims=True)
    gsel = first_argmax(gl, gmax)
    gprob = 1.0 / jnp.sum(jnp.exp(gl - gmax), axis=-1, keepdims=True)
    first = MOE_GROUPS + EXPERTS_PER_GROUP * gsel
    el = jnp.where((lane >= first) & (lane < first + EXPERTS_PER_GROUP), logits, NEG_BIG)
    m1 = jnp.max(el, axis=-1, keepdims=True)
    i1 = first_argmax(el, m1)
    el2 = jnp.where(lane == i1, NEG_BIG, el)
    m2 = jnp.max(el2, axis=-1, keepdims=True)
    i2 = first_argmax(el2, m2)
    ratio = jnp.exp(m2 - m1)
    gate1 = gprob / (1.0 + ratio)
    gate2 = gprob * ratio / (1.0 + ratio)
    a = jnp.minimum(i1, i2) - first
    bb = jnp.maximum(i1, i2) - first
    pair = (EXPERTS_PER_GROUP - 1) * a - jnp.right_shift(a * (a - 1), 1) + (bb - a - 1)
    cls = gsel * PAIRS_PER_GROUP + pair
    gate_lo = jnp.where(i1 < i2, gate1, gate2)
    gate_hi = jnp.where(i1 < i2, gate2, gate1)

    onehot = (lane == cls).astype(_F32)
    tri_r = lax.broadcasted_iota(jnp.int32, (tm, tm), 0)
    tri_c = lax.broadcasted_iota(jnp.int32, (tm, tm), 1)
    before = (tri_c < tri_r).astype(_BF16)
    seen = jnp.dot(before, onehot.astype(_BF16), preferred_element_type=_F32) + cnt_ref[...]
    rank = jnp.sum(onehot * seen, axis=-1, keepdims=True)
    cnt_ref[...] += jnp.sum(onehot, axis=0, keepdims=True)
    counts_ref[...] = cnt_ref[...]

    extra = jnp.where(lane == GATE_LO_LANE, gate_lo,
                      jnp.where(lane == GATE_HI_LANE, gate_hi,
                                jnp.where(lane == CLASS_LANE, cls.astype(_F32),
                                          jnp.where(lane == RANK_LANE, rank, 0.0))))
    xg_ref[:, 0:D_MODEL] = xn
    xg_ref[:, D_MODEL:XG_WIDTH] = extra


def _router_weights(wg, bg, we, be):
    w = jnp.zeros((D_MODEL, LANES), _F32)
    w = w.at[:, 0:MOE_GROUPS].set(wg).at[:, MOE_GROUPS:MOE_GROUPS + N_EXPERTS].set(we)
    bias = jnp.zeros((1, LANES), _F32)
    bias = bias.at[0, 0:MOE_GROUPS].set(bg).at[0, MOE_GROUPS:MOE_GROUPS + N_EXPERTS].set(be)
    w_hi = w.astype(_BF16)
    w_lo = (w - w_hi.astype(_F32)).astype(_BF16)
    return w_hi, w_lo, bias


def _post_attn_kernel(o_ref, h_ref, wo_ref, g_ref, wrh_ref, wrl_ref, br_ref,
                      h_out_ref, xg_ref, counts_ref, cnt_ref):
    @pl.when(pl.program_id(0) == 0)
    def _():
        cnt_ref[...] = jnp.zeros_like(cnt_ref)

    h_new = h_ref[...] + jnp.dot(o_ref[...], wo_ref[...], preferred_element_type=_F32)
    h_out_ref[...] = h_new
    _route(h_new, g_ref, wrh_ref, wrl_ref, br_ref, cnt_ref, xg_ref, counts_ref)


def _const_spec(shape):
    return pl.BlockSpec(shape, lambda *_: (0,) * len(shape))


def _post_attn(o2, h2, w_o, gain, wrh, wrl, br):
    t = h2.shape[0]
    tm = _pick_tile(t, (512, 256))
    row = pl.BlockSpec((tm, D_MODEL), lambda i: (i, 0))
    return pl.pallas_call(
        _post_attn_kernel,
        grid=(t // tm,),
        in_specs=[row, row, _const_spec((D_MODEL, D_MODEL)), _const_spec((1, D_MODEL)),
                  _const_spec((D_MODEL, LANES)), _const_spec((D_MODEL, LANES)), _const_spec((1, LANES))],
        out_specs=[row, pl.BlockSpec((tm, XG_WIDTH), lambda i: (i, 0)), _const_spec((1, LANES))],
        out_shape=[jax.ShapeDtypeStruct((t, D_MODEL), _F32), jax.ShapeDtypeStruct((t, XG_WIDTH), _F32),
                   jax.ShapeDtypeStruct((1, LANES), _F32)],
        scratch_shapes=[pltpu.VMEM((1, LANES), _F32)],
        compiler_params=_params("arbitrary"),
        name="attn_out_route",
    )(o2, h2, w_o, gain, wrh, wrl, br)


def _pool_kernel(h_ref, halo_ref, mg_ref, pw_ref, ps_ref, g_ref, wrh_ref, wrl_ref, br_ref,
                 h_out_ref, xg_ref, counts_ref, cnt_ref, cat_ref, *, tile):
    i = pl.program_id(1)

    @pl.when((pl.program_id(0) == 0) & (i == 0))
    def _():
        cnt_ref[...] = jnp.zeros_like(cnt_ref)

    h = h_ref[0]
    gain = mg_ref[...]
    hn = _rms(h, gain)
    cat_ref[0:POOL_HALO, :] = jnp.where(i == 0, 0.0, _rms(halo_ref[0], gain))
    cat_ref[POOL_HALO:POOL_HALO + tile, :] = hn
    token = i * tile + lax.broadcasted_iota(jnp.int32, (tile, 1), 0) - FRONT
    outs = []
    for gi, w in enumerate(POOL_WINDOWS):
        cols = slice(gi * POOL_GROUP_DIM, (gi + 1) * POOL_GROUP_DIM)
        xg = hn[:, cols]
        wsum = xg
        for d in range(1, w):
            wsum = wsum + cat_ref[pl.ds(POOL_HALO - d, tile), cols]
        count = jnp.clip(token + 1, 1, w).astype(_F32)
        mixed = wsum / count - xg
        outs.append(jnp.dot(mixed.astype(_BF16), pw_ref[gi], preferred_element_type=_F32))
    h_new = h + jnp.concatenate(outs, axis=1) * ps_ref[...]
    h_out_ref[0] = h_new
    _route(h_new, g_ref, wrh_ref, wrl_ref, br_ref, cnt_ref, xg_ref, counts_ref)


def _pool(h3, mix_gain, pool_w, pool_scale, gain, wrh, wrl, br):
    b, p, _ = h3.shape
    tile = _pick_tile(p, (768, 256))
    nt = p // tile
    halo_blocks = tile // POOL_HALO
    row = pl.BlockSpec((1, tile, D_MODEL), lambda bi, i: (bi, i, 0))
    halo = pl.BlockSpec((1, POOL_HALO, D_MODEL), lambda bi, i: (bi, jnp.maximum(i * halo_blocks - 1, 0), 0))
    h_new, xg, counts = pl.pallas_call(
        functools.partial(_pool_kernel, tile=tile),
        grid=(b, nt),
        in_specs=[row, halo, _const_spec((1, D_MODEL)),
                  _const_spec((len(POOL_WINDOWS), POOL_GROUP_DIM, POOL_GROUP_DIM)), _const_spec((1, D_MODEL)),
                  _const_spec((1, D_MODEL)), _const_spec((D_MODEL, LANES)), _const_spec((D_MODEL, LANES)),
                  _const_spec((1, LANES))],
        out_specs=[row, pl.BlockSpec((tile, XG_WIDTH), lambda bi, i: (bi * nt + i, 0)), _const_spec((1, LANES))],
        out_shape=[jax.ShapeDtypeStruct(h3.shape, _F32), jax.ShapeDtypeStruct((b * p, XG_WIDTH), _F32),
                   jax.ShapeDtypeStruct((1, LANES), _F32)],
        scratch_shapes=[pltpu.VMEM((1, LANES), _F32), pltpu.VMEM((POOL_HALO + tile, D_MODEL), _F32)],
        compiler_params=_params("arbitrary", "arbitrary"),
        name="pool_route",
    )(h3, h3, mix_gain, pool_w, pool_scale, gain, wrh, wrl, br)
    return h_new, xg, counts


def _issue_rows(tm, copy_for_row):
    def group(g, c):
        r0 = pl.multiple_of(g * SUBLANES, SUBLANES)
        for j in range(SUBLANES):
            copy_for_row(r0, j).start()
        return c

    lax.fori_loop(0, tm // SUBLANES, group, 0)


def _tile_row(ref, r0, j):
    return ref.at[pl.ds(r0, SUBLANES)].at[pl.ds(j, 1)]


def _wait_rows(tm, copy):
    def wait(r, c):
        copy.wait()
        return c

    lax.fori_loop(0, tm, wait, 0, unroll=8)


def _dispatch_kernel(dest_ref, xg_ref, out_ref, sem, *, tm):
    def copy(r0, j):
        return pltpu.make_async_copy(_tile_row(xg_ref, r0, j), out_ref.at[pl.ds(dest_ref[0, 0, r0 + j], 1)], sem)

    _issue_rows(tm, copy)
    _wait_rows(tm, copy(0, 0))


def _dispatch(xg, dest3):
    t = xg.shape[0]
    tm = dest3.shape[-1]
    return pl.pallas_call(
        functools.partial(_dispatch_kernel, tm=tm),
        grid=(t // tm,),
        in_specs=[pl.BlockSpec((1, 1, tm), lambda i: (i, 0, 0), memory_space=pltpu.SMEM),
                  pl.BlockSpec((tm, XG_WIDTH), lambda i: (i, 0))],
        out_specs=pl.BlockSpec(memory_space=pl.ANY),
        out_shape=jax.ShapeDtypeStruct(xg.shape, _F32),
        scratch_shapes=[pltpu.SemaphoreType.DMA(())],
        compiler_params=_params("arbitrary"),
        name="moe_dispatch",
    )(dest3, xg)


def _expert_kernel(blk_ref, lo_ref, hi_ref, elo_ref, ehi_ref,
                   x_ref, wg_lo, wu_lo, wd_lo, wg_hi, wu_hi, wd_hi, y_ref):
    del elo_ref, ehi_ref
    j = pl.program_id(0)

    @pl.when(jnp.logical_or(j == 0, blk_ref[jnp.maximum(j - 1, 0)] != blk_ref[j]))
    def _():
        y_ref[...] = jnp.zeros_like(y_ref)

    lo = lo_ref[j]
    hi = hi_ref[j]

    @pl.when(hi > lo)
    def _():
        row = lax.broadcasted_iota(jnp.int32, (ROW_BLOCK, 1), 0)
        inside = (row >= lo) & (row < hi)
        xb = x_ref[:, 0:D_MODEL].astype(_BF16)

        def expert(wg, wu, wd):
            gate = jnp.dot(xb, wg[0], preferred_element_type=_F32)
            up = jnp.dot(xb, wu[0], preferred_element_type=_F32)
            hid = gate * jax.nn.sigmoid(gate) * up
            return jnp.dot(hid.astype(_BF16), wd[0], preferred_element_type=_F32)

        g_lo = jnp.where(inside, x_ref[:, D_MODEL + GATE_LO_LANE:D_MODEL + GATE_LO_LANE + 1], 0.0)
        g_hi = jnp.where(inside, x_ref[:, D_MODEL + GATE_HI_LANE:D_MODEL + GATE_HI_LANE + 1], 0.0)
        y_ref[...] += expert(wg_lo, wu_lo, wd_lo) * g_lo + expert(wg_hi, wu_hi, wd_hi) * g_hi


def _experts(x_sorted, item_blk, item_lo, item_hi, item_elo, item_ehi, w_gate, w_up, w_down):
    n_items = item_blk.shape[0]

    def w_spec(shape, which):
        def index(j, blk, lo, hi, elo, ehi):
            return ((elo, ehi)[which][j], 0, 0)
        return pl.BlockSpec(shape, index)

    w_in = (1, D_MODEL, EXPERT_FF)
    w_out = (1, EXPERT_FF, D_MODEL)
    grid_spec = pltpu.PrefetchScalarGridSpec(
        num_scalar_prefetch=5,
        grid=(n_items,),
        in_specs=[pl.BlockSpec((ROW_BLOCK, XG_WIDTH), lambda j, blk, *_: (blk[j], 0)),
                  w_spec(w_in, 0), w_spec(w_in, 0), w_spec(w_out, 0),
                  w_spec(w_in, 1), w_spec(w_in, 1), w_spec(w_out, 1)],
        out_specs=pl.BlockSpec((ROW_BLOCK, D_MODEL), lambda j, blk, *_: (blk[j], 0)),
    )
    return pl.pallas_call(
        _expert_kernel,
        grid_spec=grid_spec,
        out_shape=jax.ShapeDtypeStruct((x_sorted.shape[0], D_MODEL), _F32),
        compiler_params=_params("arbitrary"),
        name="moe_experts",
    )(item_blk, item_lo, item_hi, item_elo, item_ehi, x_sorted, w_gate, w_up, w_down, w_gate, w_up, w_down)


def _combine_kernel(dest_ref, dest_next_ref, y_ref, h_ref, out_ref, rows_ref, sems, *, tm):
    i = pl.program_id(0)
    slot = i % 2

    def row_copy(d_ref, s):
        def copy(r0, j):
            src = y_ref.at[pl.ds(d_ref[0, 0, r0 + j], 1)]
            return pltpu.make_async_copy(src, _tile_row(rows_ref.at[s], r0, j), sems.at[s])
        return copy

    @pl.when(i == 0)
    def _():
        _issue_rows(tm, row_copy(dest_ref, 0))

    @pl.when(i + 1 < pl.num_programs(0))
    def _():
        _issue_rows(tm, row_copy(dest_next_ref, 1 - slot))

    _wait_rows(tm, row_copy(dest_ref, slot)(0, 0))
    out_ref[...] = h_ref[...] + rows_ref[slot]


def _combine(y_sorted, dest3, h2):
    t = h2.shape[0]
    tm = dest3.shape[-1]
    n = t // tm
    row = pl.BlockSpec((tm, D_MODEL), lambda i: (i, 0))
    return pl.pallas_call(
        functools.partial(_combine_kernel, tm=tm),
        grid=(n,),
        in_specs=[pl.BlockSpec((1, 1, tm), lambda i: (i, 0, 0), memory_space=pltpu.SMEM),
                  pl.BlockSpec((1, 1, tm), lambda i: (jnp.minimum(i + 1, n - 1), 0, 0), memory_space=pltpu.SMEM),
                  pl.BlockSpec(memory_space=pl.ANY), row],
        out_specs=row,
        out_shape=jax.ShapeDtypeStruct(h2.shape, _F32),
        scratch_shapes=[pltpu.VMEM((2, tm, D_MODEL), _F32), pltpu.SemaphoreType.DMA((2,))],
        input_output_aliases={3: 0},
        compiler_params=_params("arbitrary"),
        name="moe_combine",
    )(dest3, dest3, y_sorted, h2)


def _moe(h2, xg, counts, w_gate, w_up, w_down):
    t = h2.shape[0]
    tm = _pick_tile(t, (512, 256))
    n_blocks = t // ROW_BLOCK
    cnt = counts[0, :N_CLASSES].astype(jnp.int32)
    ends = jnp.cumsum(cnt)
    starts = ends - cnt
    cuts = jnp.sort(jnp.concatenate([jnp.arange(n_blocks, dtype=jnp.int32) * ROW_BLOCK, starts[1:]]))
    item_blk = jnp.minimum(cuts // ROW_BLOCK, n_blocks - 1)
    item_lo = cuts - item_blk * ROW_BLOCK
    item_hi = jnp.concatenate([cuts[1:], jnp.full((1,), t, jnp.int32)]) - item_blk * ROW_BLOCK
    item_cls = jnp.minimum(jnp.searchsorted(ends, cuts, side="right"), N_CLASSES - 1)
    item_elo = jnp.asarray(_CLASS_LO)[item_cls]
    item_ehi = jnp.asarray(_CLASS_HI)[item_cls]
    cls = xg[:, D_MODEL + CLASS_LANE].astype(jnp.int32)
    rank = xg[:, D_MODEL + RANK_LANE].astype(jnp.int32)
    class_ids = jnp.arange(N_CLASSES, dtype=jnp.int32)
    start_of = jnp.sum(jnp.where(cls[:, None] == class_ids[None, :], starts[None, :], 0), axis=1)
    dest3 = (start_of + rank).astype(jnp.int32).reshape(t // tm, 1, tm)
    x_sorted = _dispatch(xg, dest3)
    y_sorted = _experts(x_sorted, item_blk, item_lo, item_hi, item_elo, item_ehi, w_gate, w_up, w_down)
    return _combine(y_sorted, dest3, h2)


def _final_kernel(h_ref, g_ref, o_ref):
    o_ref[0] = _rms(h_ref[0], g_ref[...])


def _final(h3, gain, seq):
    b = h3.shape[0]
    return pl.pallas_call(
        _final_kernel,
        grid=(b, seq // ATTN_TILE),
        in_specs=[pl.BlockSpec((1, ATTN_TILE, D_MODEL), lambda bi, i: (bi, i + 1, 0)), _const_spec((1, D_MODEL))],
        out_specs=pl.BlockSpec((1, ATTN_TILE, D_MODEL), lambda bi, i: (bi, i, 0)),
        out_shape=jax.ShapeDtypeStruct((b, seq, D_MODEL), _F32),
        compiler_params=_params("parallel", "parallel"),
        name="final_norm",
    )(h3, gain)


def kernel(x, meta_tokens, mix_norm, ffn_norm, final_norm, sb_w_qkv, sb_w_o, pool_w, pool_scale,
           router_group_w, router_group_b, router_expert_w, router_expert_b,
           expert_w_gate, expert_w_up, expert_w_down):
    b, seq, d = x.shape
    assert d == D_MODEL and seq % ATTN_TILE == 0
    depth = mix_norm.shape[0]
    p = seq + ATTN_TILE
    t = b * p
    meta = jnp.broadcast_to(meta_tokens[None].astype(x.dtype), (b, N_META, d))
    h3 = jnp.concatenate([jnp.zeros((b, FRONT, d), x.dtype), meta, x], axis=1)

    for i in range(depth):
        j = i // 2
        wrh, wrl, br = _router_weights(router_group_w[i], router_group_b[i], router_expert_w[i], router_expert_b[i])
        ffn_gain = ffn_norm[i].reshape(1, d)
        mix_gain = mix_norm[i].reshape(1, d)
        if i % 2 == 0:
            q, k, v = _qkv(h3.reshape(t, d), mix_gain, sb_w_qkv[j].astype(_BF16))
            o = _attention(q.reshape(b, p, d), k.reshape(b, p, d), v.reshape(b, p, d))
            h2, xg, counts = _post_attn(o.reshape(t, d), h3.reshape(t, d), sb_w_o[j].astype(_BF16),
                                        ffn_gain, wrh, wrl, br)
        else:
            h3, xg, counts = _pool(h3, mix_gain, pool_w[j].astype(_BF16), pool_scale[j].reshape(1, d),
                                   ffn_gain, wrh, wrl, br)
            h2 = h3.reshape(t, d)
        h2 = _moe(h2, xg, counts, expert_w_gate[i].astype(_BF16), expert_w_up[i].astype(_BF16),
                  expert_w_down[i].astype(_BF16))
        h3 = h2.reshape(b, p, d)
    return _final(h3, final_norm.reshape(1, d), seq)
```

```python
import functools

import numpy as np
import jax
import jax.numpy as jnp
from jax import lax
from jax.experimental import pallas as pl
from jax.experimental.pallas import tpu as pltpu

D_MODEL = 1024
N_META = 16
HEADS = 8
HEAD_DIM = D_MODEL // HEADS
POOL_WINDOWS = (2, 4, 8, 16)
POOL_GROUP_DIM = D_MODEL // len(POOL_WINDOWS)
POOL_HALO = 16
MOE_GROUPS = 4
EXPERTS_PER_GROUP = 8
N_EXPERTS = MOE_GROUPS * EXPERTS_PER_GROUP
EXPERT_FF = D_MODEL // 4
PAIRS_PER_GROUP = EXPERTS_PER_GROUP * (EXPERTS_PER_GROUP - 1) // 2
N_CLASSES = MOE_GROUPS * PAIRS_PER_GROUP
RMS_EPS = 1e-6

LANES = 128
SUBLANES = 8
ATTN_TILE = 256
FRONT = ATTN_TILE - N_META
HEADS_PER_STEP = 8
LOG2_E = 1.4426950408889634
EXHAUSTED_LOG2 = -200.0
ROW_BLOCK = 256
XG_WIDTH = D_MODEL + LANES
GATE_LO_LANE, GATE_HI_LANE, CLASS_LANE, RANK_LANE = 0, 1, 2, 3
NEG_BIG = -1e30
VMEM_LIMIT = 48 * 1024 * 1024

_F32 = jnp.float32
_BF16 = jnp.bfloat16


def _class_tables():
    lo, hi = [], []
    for g in range(MOE_GROUPS):
        for a in range(EXPERTS_PER_GROUP):
            for b in range(a + 1, EXPERTS_PER_GROUP):
                lo.append(g * EXPERTS_PER_GROUP + a)
                hi.append(g * EXPERTS_PER_GROUP + b)
    return np.asarray(lo, np.int32), np.asarray(hi, np.int32)


_CLASS_LO, _CLASS_HI = _class_tables()


def _pick_tile(n, candidates):
    for c in candidates:
        if n % c == 0:
            return c
    raise ValueError(f"no tile in {candidates} divides {n}")


def _params(*semantics):
    return pltpu.CompilerParams(dimension_semantics=semantics, vmem_limit_bytes=VMEM_LIMIT)


def _rms(x, g):
    return x * lax.rsqrt(jnp.mean(x * x, axis=-1, keepdims=True) + RMS_EPS) * g


def _qkv_kernel(h_ref, g_ref, w_ref, q_ref, k_ref, v_ref):
    _qkv_body(h_ref[...], g_ref, w_ref, q_ref, k_ref, v_ref)


def _qkv_body(h, g_ref, w_ref, q_ref, k_ref, v_ref):
    xn = _rms(h, g_ref[...]).astype(_BF16)
    scale = HEAD_DIM ** -0.5 * LOG2_E
    q = jnp.dot(xn, w_ref[:, 0:D_MODEL], preferred_element_type=_F32)
    q_ref[...] = (q * scale).astype(q_ref.dtype)
    k = jnp.dot(xn, w_ref[:, D_MODEL:2 * D_MODEL], preferred_element_type=_F32)
    k_ref[...] = k.astype(k_ref.dtype)
    v = jnp.dot(xn, w_ref[:, 2 * D_MODEL:3 * D_MODEL], preferred_element_type=_F32)
    v_ref[...] = v.astype(v_ref.dtype)


def _qkv(h2, gain, w_qkv):
    t = h2.shape[0]
    tm = _pick_tile(t, (512, 256))
    row = pl.BlockSpec((tm, D_MODEL), lambda i: (i, 0))
    out = jax.ShapeDtypeStruct((t, D_MODEL), _BF16)
    return pl.pallas_call(
        _qkv_kernel,
        grid=(t // tm,),
        in_specs=[row, pl.BlockSpec((1, D_MODEL), lambda i: (0, 0)),
                  pl.BlockSpec((D_MODEL, 3 * D_MODEL), lambda i: (0, 0))],
        out_specs=[row, row, row],
        out_shape=[out, out, out],
        compiler_params=_params("parallel"),
        name="qkv_proj",
    )(h2, gain, w_qkv)


def _neg_abs(x):
    bits = lax.bitcast_convert_type(x, jnp.uint32) | jnp.uint32(0x80000000)
    return lax.bitcast_convert_type(bits, _F32)


def _attn_kernel(q_ref, k_ref, v_ref, o_ref, acc_ref, carry_ref):
    qi = pl.program_id(2)
    tri_r = lax.broadcasted_iota(jnp.int32, (ATTN_TILE, ATTN_TILE), 0)
    tri_c = lax.broadcasted_iota(jnp.int32, (ATTN_TILE, ATTN_TILE), 1)
    neg_suffix = jnp.where(tri_r > tri_c, -1.0, 0.0).astype(_BF16)
    acc_ref[...] = jnp.zeros_like(acc_ref)
    carry_ref[...] = jnp.zeros_like(carry_ref)

    def step(kb, masked):
        start = pl.multiple_of(kb * ATTN_TILE, ATTN_TILE)
        if masked:
            q_pos = qi * ATTN_TILE + tri_r
            k_pos = kb * ATTN_TILE + tri_c
            mask = (k_pos < q_pos) & (k_pos >= FRONT)
        for g in range(HEADS_PER_STEP):
            cols = slice(g * HEAD_DIM, (g + 1) * HEAD_DIM)
            k = k_ref[0, pl.ds(start, ATTN_TILE), cols]
            v = v_ref[0, pl.ds(start, ATTN_TILE), cols]
            z = lax.dot_general(q_ref[0, :, cols], k, (((1,), (1,)), ((), ())), preferred_element_type=_F32)
            sp = jnp.maximum(z, 0.0) + jnp.log(1.0 + jnp.exp2(_neg_abs(z))) * LOG2_E
            if masked:
                sp = jnp.where(mask, sp, 0.0)
            within = jnp.dot(sp.astype(_BF16), neg_suffix, preferred_element_type=_F32)
            later = within + carry_ref[g]
            a = jnp.exp2(z - sp + later)
            if masked:
                a = jnp.where(mask, a, 0.0)
            acc_ref[g] += jnp.dot(a.astype(_BF16), v, preferred_element_type=_F32)
            carry_ref[g] = later[:, 0:1] - sp[:, 0:1]

    def alive():
        top = carry_ref[0]
        for g in range(1, HEADS_PER_STEP):
            top = jnp.maximum(top, carry_ref[g])
        return (jnp.max(top) > EXHAUSTED_LOG2).astype(jnp.int32)

    step(qi, True)

    def cond(state):
        kb, live = state
        return jnp.logical_and(kb >= 1, live > 0)

    def body(state):
        kb, _ = state
        step(kb, False)
        return kb - 1, alive()

    kb_end, live = lax.while_loop(cond, body, (qi - 1, alive()))

    @pl.when(jnp.logical_and(kb_end == 0, live > 0))
    def _():
        step(0, True)

    for g in range(HEADS_PER_STEP):
        o_ref[0, :, g * HEAD_DIM:(g + 1) * HEAD_DIM] = acc_ref[g].astype(o_ref.dtype)


def _attention(q, k, v):
    b, p, _ = q.shape
    nq = p // ATTN_TILE
    width = HEADS_PER_STEP * HEAD_DIM
    qspec = pl.BlockSpec((1, ATTN_TILE, width), lambda bi, hi, qi: (bi, qi, hi))
    kvspec = pl.BlockSpec((1, p, width), lambda bi, hi, qi: (bi, 0, hi), pipeline_mode=pl.Buffered(1))
    return pl.pallas_call(
        _attn_kernel,
        grid=(b, HEADS // HEADS_PER_STEP, nq),
        in_specs=[qspec, kvspec, kvspec],
        out_specs=qspec,
        out_shape=jax.ShapeDtypeStruct(q.shape, _BF16),
        scratch_shapes=[pltpu.VMEM((HEADS_PER_STEP, ATTN_TILE, HEAD_DIM), _F32),
                        pltpu.VMEM((HEADS_PER_STEP, ATTN_TILE, 1), _F32)],
        compiler_params=_params("parallel", "parallel", "arbitrary"),
        name="sb_attention",
    )(q, k, v)


def _route(h_new, g_ref, wr_ref, br_ref, cnt_ref, xg_ref, counts_ref):
    tm = h_new.shape[0]
    xn = _rms(h_new, g_ref[...])
    x_hi = xn.astype(_BF16)
    x_lo = (xn - x_hi.astype(_F32)).astype(_BF16)
    both = jnp.dot(x_hi, wr_ref[...], preferred_element_type=_F32)
    logits = (both[:, 0:LANES] + both[:, LANES:2 * LANES]
              + jnp.dot(x_lo, wr_ref[:, 0:LANES], preferred_element_type=_F32)) + br_ref[...]
    lane = lax.broadcasted_iota(jnp.int32, (tm, LANES), 1).astype(_F32)

    def first_argmax(vals, vmax):
        return jnp.min(jnp.where(vals >= vmax, lane, float(LANES)), axis=-1, keepdims=True)

    gl = jnp.where(lane < MOE_GROUPS, logits, NEG_BIG)
    gmax = jnp.max(gl, axis=-1, keepdims=True)
    gsel = first_argmax(gl, gmax)
    gprob = 1.0 / jnp.sum(jnp.exp(gl - gmax), axis=-1, keepdims=True)
    first = MOE_GROUPS + EXPERTS_PER_GROUP * gsel
    el = jnp.where((lane >= first) & (lane < first + EXPERTS_PER_GROUP), logits, NEG_BIG)
    m1 = jnp.max(el, axis=-1, keepdims=True)
    i1 = first_argmax(el, m1)
    el2 = jnp.where(lane == i1, NEG_BIG, el)
    m2 = jnp.max(el2, axis=-1, keepdims=True)
    i2 = first_argmax(el2, m2)
    ratio = jnp.exp(m2 - m1)
    gate1 = gprob / (1.0 + ratio)
    gate2 = gprob * ratio / (1.0 + ratio)
    a = jnp.minimum(i1, i2) - first
    bb = jnp.maximum(i1, i2) - first
    pair = (EXPERTS_PER_GROUP - 1) * a - a * (a - 1.0) * 0.5 + (bb - a - 1.0)
    cls = gsel * PAIRS_PER_GROUP + pair
    gate_lo = jnp.where(i1 < i2, gate1, gate2)
    gate_hi = jnp.where(i1 < i2, gate2, gate1)

    onehot = (lane == cls).astype(_F32)
    tri_r = lax.broadcasted_iota(jnp.int32, (tm, tm), 0)
    tri_c = lax.broadcasted_iota(jnp.int32, (tm, tm), 1)
    before = (tri_c < tri_r).astype(_BF16)
    seen = jnp.dot(before, onehot.astype(_BF16), preferred_element_type=_F32) + cnt_ref[...]
    rank = jnp.sum(onehot * seen, axis=-1, keepdims=True)
    cnt_ref[...] += jnp.sum(onehot, axis=0, keepdims=True)
    counts_ref[...] = cnt_ref[...]

    extra = jnp.where(lane == GATE_LO_LANE, gate_lo,
                      jnp.where(lane == GATE_HI_LANE, gate_hi,
                                jnp.where(lane == CLASS_LANE, cls,
                                          jnp.where(lane == RANK_LANE, rank, 0.0))))
    xg_ref[:, 0:D_MODEL] = xn
    xg_ref[:, D_MODEL:XG_WIDTH] = extra


def _router_weights(wg, bg, we, be):
    w = jnp.zeros((D_MODEL, LANES), _F32)
    w = w.at[:, 0:MOE_GROUPS].set(wg).at[:, MOE_GROUPS:MOE_GROUPS + N_EXPERTS].set(we)
    bias = jnp.zeros((1, LANES), _F32)
    bias = bias.at[0, 0:MOE_GROUPS].set(bg).at[0, MOE_GROUPS:MOE_GROUPS + N_EXPERTS].set(be)
    w_hi = w.astype(_BF16)
    w_lo = (w - w_hi.astype(_F32)).astype(_BF16)
    return jnp.concatenate([w_hi, w_lo], axis=1), bias


def _post_attn_kernel(o_ref, h_ref, wo_ref, g_ref, wr_ref, br_ref,
                      h_out_ref, xg_ref, counts_ref, cnt_ref):
    @pl.when(pl.program_id(0) == 0)
    def _():
        cnt_ref[...] = jnp.zeros_like(cnt_ref)

    h_new = h_ref[...] + jnp.dot(o_ref[...], wo_ref[...], preferred_element_type=_F32)
    h_out_ref[...] = h_new
    _route(h_new, g_ref, wr_ref, br_ref, cnt_ref, xg_ref, counts_ref)


def _const_spec(shape):
    return pl.BlockSpec(shape, lambda *_: (0,) * len(shape))


def _post_attn(o2, h2, w_o, gain, wr, br):
    t = h2.shape[0]
    tm = _pick_tile(t, (512, 256))
    row = pl.BlockSpec((tm, D_MODEL), lambda i: (i, 0))
    return pl.pallas_call(
        _post_attn_kernel,
        grid=(t // tm,),
        in_specs=[row, row, _const_spec((D_MODEL, D_MODEL)), _const_spec((1, D_MODEL)),
                  _const_spec((D_MODEL, 2 * LANES)), _const_spec((1, LANES))],
        out_specs=[row, pl.BlockSpec((tm, XG_WIDTH), lambda i: (i, 0)), _const_spec((1, LANES))],
        out_shape=[jax.ShapeDtypeStruct((t, D_MODEL), _F32), jax.ShapeDtypeStruct((t, XG_WIDTH), _F32),
                   jax.ShapeDtypeStruct((1, LANES), _F32)],
        scratch_shapes=[pltpu.VMEM((1, LANES), _F32)],
        compiler_params=_params("arbitrary"),
        name="attn_out_route",
    )(o2, h2, w_o, gain, wr, br)


def _pool_kernel(h_ref, halo_ref, mg_ref, pw_ref, ps_ref, g_ref, wr_ref, br_ref,
                 h_out_ref, xg_ref, counts_ref, cnt_ref, *, tile):
    i = pl.program_id(1)

    @pl.when((pl.program_id(0) == 0) & (i == 0))
    def _():
        cnt_ref[...] = jnp.zeros_like(cnt_ref)

    h = h_ref[0]
    gain = mg_ref[...]
    hn = _rms(h, gain)
    halo = jnp.where(i == 0, 0.0, _rms(halo_ref[0], gain))
    cat = jnp.concatenate([halo, hn], axis=0)
    token = i * tile + lax.broadcasted_iota(jnp.int32, (tile, 1), 0) - FRONT
    outs = []
    for gi, w in enumerate(POOL_WINDOWS):
        cols = slice(gi * POOL_GROUP_DIM, (gi + 1) * POOL_GROUP_DIM)
        xg = hn[:, cols]
        run = cat[:, cols]
        shift = 1
        while shift < w:
            run = run + pltpu.roll(run, shift, axis=0)
            shift *= 2
        wsum = run[POOL_HALO:, :]
        count = jnp.clip(token + 1, 1, w).astype(_F32)
        mixed = wsum / count - xg
        outs.append(jnp.dot(mixed.astype(_BF16), pw_ref[gi], preferred_element_type=_F32))
    h_new = h + jnp.concatenate(outs, axis=1) * ps_ref[...]
    h_out_ref[0] = h_new
    _route(h_new, g_ref, wr_ref, br_ref, cnt_ref, xg_ref, counts_ref)


def _pool(h3, mix_gain, pool_w, pool_scale, gain, wr, br):
    b, p, _ = h3.shape
    tile = _pick_tile(p, (768, 256))
    nt = p // tile
    halo_blocks = tile // POOL_HALO
    row = pl.BlockSpec((1, tile, D_MODEL), lambda bi, i: (bi, i, 0))
    halo = pl.BlockSpec((1, POOL_HALO, D_MODEL), lambda bi, i: (bi, jnp.maximum(i * halo_blocks - 1, 0), 0))
    h_new, xg, counts = pl.pallas_call(
        functools.partial(_pool_kernel, tile=tile),
        grid=(b, nt),
        in_specs=[row, halo, _const_spec((1, D_MODEL)),
                  _const_spec((len(POOL_WINDOWS), POOL_GROUP_DIM, POOL_GROUP_DIM)), _const_spec((1, D_MODEL)),
                  _const_spec((1, D_MODEL)), _const_spec((D_MODEL, 2 * LANES)), _const_spec((1, LANES))],
        out_specs=[row, pl.BlockSpec((tile, XG_WIDTH), lambda bi, i: (bi * nt + i, 0)), _const_spec((1, LANES))],
        out_shape=[jax.ShapeDtypeStruct(h3.shape, _F32), jax.ShapeDtypeStruct((b * p, XG_WIDTH), _F32),
                   jax.ShapeDtypeStruct((1, LANES), _F32)],
        scratch_shapes=[pltpu.VMEM((1, LANES), _F32)],
        compiler_params=_params("arbitrary", "arbitrary"),
        name="pool_route",
    )(h3, h3, mix_gain, pool_w, pool_scale, gain, wr, br)
    return h_new, xg, counts


def _issue_rows(tm, copy_for_row):
    def group(g, c):
        r0 = pl.multiple_of(g * SUBLANES, SUBLANES)
        for j in range(SUBLANES):
            copy_for_row(r0, j).start()
        return c

    lax.fori_loop(0, tm // SUBLANES, group, 0)


def _tile_row(ref, r0, j):
    return ref.at[pl.ds(r0, SUBLANES)].at[pl.ds(j, 1)]


def _wait_rows(tm, copy):
    def wait(r, c):
        copy.wait()
        return c

    lax.fori_loop(0, tm, wait, 0, unroll=8)


def _dispatch_kernel(dest_ref, xg_ref, out_ref, sem, *, tm):
    def copy(r0, j):
        return pltpu.make_async_copy(_tile_row(xg_ref, r0, j), out_ref.at[pl.ds(dest_ref[0, 0, r0 + j], 1)], sem)

    _issue_rows(tm, copy)
    _wait_rows(tm, copy(0, 0))


def _dispatch(xg, dest3):
    t = xg.shape[0]
    tm = dest3.shape[-1]
    return pl.pallas_call(
        functools.partial(_dispatch_kernel, tm=tm),
        grid=(t // tm,),
        in_specs=[pl.BlockSpec((1, 1, tm), lambda i: (i, 0, 0), memory_space=pltpu.SMEM),
                  pl.BlockSpec((tm, XG_WIDTH), lambda i: (i, 0))],
        out_specs=pl.BlockSpec(memory_space=pl.ANY),
        out_shape=jax.ShapeDtypeStruct(xg.shape, _F32),
        scratch_shapes=[pltpu.SemaphoreType.DMA(())],
        compiler_params=_params("arbitrary"),
        name="moe_dispatch",
    )(dest3, xg)


def _expert_kernel(blk_ref, lo_ref, hi_ref, elo_ref, ehi_ref,
                   x_ref, wg_lo, wu_lo, wd_lo, wg_hi, wu_hi, wd_hi, y_ref):
    del elo_ref, ehi_ref
    j = pl.program_id(0)

    @pl.when(jnp.logical_or(j == 0, blk_ref[jnp.maximum(j - 1, 0)] != blk_ref[j]))
    def _():
        y_ref[...] = jnp.zeros_like(y_ref)

    lo = lo_ref[j]
    hi = hi_ref[j]

    @pl.when(hi > lo)
    def _():
        row = lax.broadcasted_iota(jnp.int32, (ROW_BLOCK, 1), 0)
        inside = (row >= lo) & (row < hi)
        xb = x_ref[:, 0:D_MODEL].astype(_BF16)

        def expert(wg, wu, wd):
            gate = jnp.dot(xb, wg[0], preferred_element_type=_F32)
            up = jnp.dot(xb, wu[0], preferred_element_type=_F32)
            hid = gate * jax.nn.sigmoid(gate) * up
            return jnp.dot(hid.astype(_BF16), wd[0], preferred_element_type=_F32)

        g_lo = jnp.where(inside, x_ref[:, D_MODEL + GATE_LO_LANE:D_MODEL + GATE_LO_LANE + 1], 0.0)
        g_hi = jnp.where(inside, x_ref[:, D_MODEL + GATE_HI_LANE:D_MODEL + GATE_HI_LANE + 1], 0.0)
        y_ref[...] += expert(wg_lo, wu_lo, wd_lo) * g_lo + expert(wg_hi, wu_hi, wd_hi) * g_hi


def _experts(x_sorted, item_blk, item_lo, item_hi, item_elo, item_ehi, w_gate, w_up, w_down):
    n_items = item_blk.shape[0]

    def w_spec(shape, which):
        def index(j, blk, lo, hi, elo, ehi):
            return ((elo, ehi)[which][j], 0, 0)
        return pl.BlockSpec(shape, index)

    w_in = (1, D_MODEL, EXPERT_FF)
    w_out = (1, EXPERT_FF, D_MODEL)
    grid_spec = pltpu.PrefetchScalarGridSpec(
        num_scalar_prefetch=5,
        grid=(n_items,),
        in_specs=[pl.BlockSpec((ROW_BLOCK, XG_WIDTH), lambda j, blk, *_: (blk[j], 0)),
                  w_spec(w_in, 0), w_spec(w_in, 0), w_spec(w_out, 0),
                  w_spec(w_in, 1), w_spec(w_in, 1), w_spec(w_out, 1)],
        out_specs=pl.BlockSpec((ROW_BLOCK, D_MODEL), lambda j, blk, *_: (blk[j], 0)),
    )
    return pl.pallas_call(
        _expert_kernel,
        grid_spec=grid_spec,
        out_shape=jax.ShapeDtypeStruct((x_sorted.shape[0], D_MODEL), _F32),
        compiler_params=_params("arbitrary"),
        name="moe_experts",
    )(item_blk, item_lo, item_hi, item_elo, item_ehi, x_sorted, w_gate, w_up, w_down, w_gate, w_up, w_down)


def _expert_rows(dest_ref, dest_next_ref, y_ref, rows_ref, sems, tm):
    i = pl.program_id(0)
    slot = i % 2

    def row_copy(d_ref, s):
        def copy(r0, j):
            src = y_ref.at[pl.ds(d_ref[0, 0, r0 + j], 1)]
            return pltpu.make_async_copy(src, _tile_row(rows_ref.at[s], r0, j), sems.at[s])
        return copy

    @pl.when(i == 0)
    def _():
        _issue_rows(tm, row_copy(dest_ref, 0))

    @pl.when(i + 1 < pl.num_programs(0))
    def _():
        _issue_rows(tm, row_copy(dest_next_ref, 1 - slot))

    _wait_rows(tm, row_copy(dest_ref, slot)(0, 0))
    return rows_ref[slot]


def _expert_rows_specs(n_steps, tm, tile_of_step):
    def dest(offset):
        return pl.BlockSpec((1, 1, tm), lambda i: (tile_of_step(jnp.minimum(i + offset, n_steps - 1)), 0, 0),
                            memory_space=pltpu.SMEM)
    return [dest(0), dest(1), pl.BlockSpec(memory_space=pl.ANY)]


def _expert_rows_scratch(tm):
    return [pltpu.VMEM((2, tm, D_MODEL), _F32), pltpu.SemaphoreType.DMA((2,))]


def _combine_kernel(dest_ref, dest_next_ref, y_ref, h_ref, out_ref, rows_ref, sems, *, tm):
    out_ref[...] = h_ref[...] + _expert_rows(dest_ref, dest_next_ref, y_ref, rows_ref, sems, tm)


def _combine(y_sorted, dest, h2):
    t = h2.shape[0]
    tm = _pick_tile(t, (512, 256))
    n = t // tm
    dest3 = dest.reshape(n, 1, tm)
    row = pl.BlockSpec((tm, D_MODEL), lambda i: (i, 0))
    return pl.pallas_call(
        functools.partial(_combine_kernel, tm=tm),
        grid=(n,),
        in_specs=_expert_rows_specs(n, tm, lambda s: s) + [row],
        out_specs=row,
        out_shape=jax.ShapeDtypeStruct(h2.shape, _F32),
        scratch_shapes=_expert_rows_scratch(tm),
        input_output_aliases={3: 0},
        compiler_params=_params("arbitrary"),
        name="moe_combine",
    )(dest3, dest3, y_sorted, h2)


def _qkv_combine_kernel(dest_ref, dest_next_ref, y_ref, h_ref, g_ref, w_ref,
                        h_out_ref, q_ref, k_ref, v_ref, rows_ref, sems, *, tm):
    h = h_ref[...] + _expert_rows(dest_ref, dest_next_ref, y_ref, rows_ref, sems, tm)
    h_out_ref[...] = h
    _qkv_body(h, g_ref, w_ref, q_ref, k_ref, v_ref)


def _qkv_combine(y_sorted, dest, h2, gain, w_qkv):
    t = h2.shape[0]
    tm = _pick_tile(t, (512, 256))
    n = t // tm
    dest3 = dest.reshape(n, 1, tm)
    row = pl.BlockSpec((tm, D_MODEL), lambda i: (i, 0))
    out = jax.ShapeDtypeStruct((t, D_MODEL), _BF16)
    return pl.pallas_call(
        functools.partial(_qkv_combine_kernel, tm=tm),
        grid=(n,),
        in_specs=_expert_rows_specs(n, tm, lambda s: s)
        + [row, _const_spec((1, D_MODEL)), _const_spec((D_MODEL, 3 * D_MODEL))],
        out_specs=[row, row, row, row],
        out_shape=[jax.ShapeDtypeStruct((t, D_MODEL), _F32), out, out, out],
        scratch_shapes=_expert_rows_scratch(tm),
        input_output_aliases={3: 0},
        compiler_params=_params("arbitrary"),
        name="combine_qkv_proj",
    )(dest3, dest3, y_sorted, h2, gain, w_qkv)


def _moe(xg, counts, w_gate, w_up, w_down):
    t = xg.shape[0]
    tm = _pick_tile(t, (512, 256))
    n_blocks = t // ROW_BLOCK
    cnt = counts[0, :N_CLASSES].astype(jnp.int32)
    ends = jnp.cumsum(cnt)
    starts = ends - cnt
    cuts = jnp.sort(jnp.concatenate([jnp.arange(n_blocks, dtype=jnp.int32) * ROW_BLOCK, starts[1:]]))
    item_blk = jnp.minimum(cuts // ROW_BLOCK, n_blocks - 1)
    item_lo = cuts - item_blk * ROW_BLOCK
    item_hi = jnp.concatenate([cuts[1:], jnp.full((1,), t, jnp.int32)]) - item_blk * ROW_BLOCK
    item_cls = jnp.minimum(jnp.searchsorted(ends, cuts, side="right"), N_CLASSES - 1)
    item_elo = jnp.asarray(_CLASS_LO)[item_cls]
    item_ehi = jnp.asarray(_CLASS_HI)[item_cls]
    cls = xg[:, D_MODEL + CLASS_LANE].astype(jnp.int32)
    rank = xg[:, D_MODEL + RANK_LANE].astype(jnp.int32)
    class_ids = jnp.arange(N_CLASSES, dtype=jnp.int32)
    start_of = jnp.sum(jnp.where(cls[:, None] == class_ids[None, :], starts[None, :], 0), axis=1)
    dest = (start_of + rank).astype(jnp.int32)
    x_sorted = _dispatch(xg, dest.reshape(t // tm, 1, tm))
    y_sorted = _experts(x_sorted, item_blk, item_lo, item_hi, item_elo, item_ehi, w_gate, w_up, w_down)
    return y_sorted, dest


def _final_kernel(dest_ref, dest_next_ref, y_ref, h_ref, g_ref, o_ref, rows_ref, sems, *, tm):
    h = h_ref[...] + _expert_rows(dest_ref, dest_next_ref, y_ref, rows_ref, sems, tm)
    o_ref[...] = _rms(h, g_ref[...])


def _final(y_sorted, dest, h2, gain, b, seq):
    tm = ATTN_TILE
    per_batch = seq // tm
    n = b * per_batch

    def tile_of_step(s):
        return (s // per_batch) * (per_batch + 1) + s % per_batch + 1

    out = pl.pallas_call(
        functools.partial(_final_kernel, tm=tm),
        grid=(n,),
        in_specs=_expert_rows_specs(n, tm, tile_of_step)
        + [pl.BlockSpec((tm, D_MODEL), lambda i: (tile_of_step(i), 0)), _const_spec((1, D_MODEL))],
        out_specs=pl.BlockSpec((tm, D_MODEL), lambda i: (i, 0)),
        out_shape=jax.ShapeDtypeStruct((b * seq, D_MODEL), _F32),
        scratch_shapes=_expert_rows_scratch(tm),
        compiler_params=_params("arbitrary"),
        name="final_norm",
    )(dest.reshape(-1, 1, tm), dest.reshape(-1, 1, tm), y_sorted, h2, gain)
    return out.reshape(b, seq, D_MODEL)


def kernel(x, meta_tokens, mix_norm, ffn_norm, final_norm, sb_w_qkv, sb_w_o, pool_w, pool_scale,
           router_group_w, router_group_b, router_expert_w, router_expert_b,
           expert_w_gate, expert_w_up, expert_w_down):
    b, seq, d = x.shape
    assert d == D_MODEL and seq % ATTN_TILE == 0
    depth = mix_norm.shape[0]
    p = seq + ATTN_TILE
    t = b * p
    meta = jnp.broadcast_to(meta_tokens[None].astype(x.dtype), (b, N_META, d))
    h2 = jnp.concatenate([jnp.zeros((b, FRONT, d), x.dtype), meta, x], axis=1).reshape(t, d)
    moe_out = None

    for i in range(depth):
        j = i // 2
        wr, br = _router_weights(router_group_w[i], router_group_b[i], router_expert_w[i], router_expert_b[i])
        ffn_gain = ffn_norm[i].reshape(1, d)
        mix_gain = mix_norm[i].reshape(1, d)
        if i % 2 == 0:
            w_qkv = sb_w_qkv[j].astype(_BF16)
            if moe_out is None:
                q, k, v = _qkv(h2, mix_gain, w_qkv)
            else:
                h2, q, k, v = _qkv_combine(*moe_out, h2, mix_gain, w_qkv)
            o = _attention(q.reshape(b, p, d), k.reshape(b, p, d), v.reshape(b, p, d))
            h2, xg, counts = _post_attn(o.reshape(t, d), h2, sb_w_o[j].astype(_BF16), ffn_gain, wr, br)
        else:
            if moe_out is not None:
                h2 = _combine(*moe_out, h2)
            h3, xg, counts = _pool(h2.reshape(b, p, d), mix_gain, pool_w[j].astype(_BF16),
                                   pool_scale[j].reshape(1, d), ffn_gain, wr, br)
            h2 = h3.reshape(t, d)
        moe_out = _moe(xg, counts, expert_w_gate[i].astype(_BF16), expert_w_up[i].astype(_BF16),
                       expert_w_down[i].astype(_BF16))
    return _final(*moe_out, h2, final_norm.reshape(1, d), b, seq)
```

```python
import functools

import numpy as np
import jax
import jax.numpy as jnp
from jax import lax
from jax.experimental import pallas as pl
from jax.experimental.pallas import tpu as pltpu

D_MODEL = 1024
N_META = 16
HEADS = 8
HEAD_DIM = D_MODEL // HEADS
POOL_WINDOWS = (2, 4, 8, 16)
POOL_GROUP_DIM = D_MODEL // len(POOL_WINDOWS)
POOL_HALO = 16
MOE_GROUPS = 4
EXPERTS_PER_GROUP = 8
N_EXPERTS = MOE_GROUPS * EXPERTS_PER_GROUP
EXPERT_FF = D_MODEL // 4
PAIRS_PER_GROUP = EXPERTS_PER_GROUP * (EXPERTS_PER_GROUP - 1) // 2
N_CLASSES = MOE_GROUPS * PAIRS_PER_GROUP
RMS_EPS = 1e-6

LANES = 128
SUBLANES = 8
ATTN_TILE = 256
FRONT = ATTN_TILE - N_META
HEADS_PER_STEP = 8
LOG2_E = 1.4426950408889634
EXHAUSTED_LOG2 = -200.0
ROW_BLOCK = 256
XG_WIDTH = D_MODEL + LANES
GATE_LO_LANE, GATE_HI_LANE, CLASS_LANE, RANK_LANE = 0, 1, 2, 3
NEG_BIG = -1e30
VMEM_LIMIT = 48 * 1024 * 1024

_F32 = jnp.float32
_BF16 = jnp.bfloat16


def _class_tables():
    lo, hi = [], []
    for g in range(MOE_GROUPS):
        for a in range(EXPERTS_PER_GROUP):
            for b in range(a + 1, EXPERTS_PER_GROUP):
                lo.append(g * EXPERTS_PER_GROUP + a)
                hi.append(g * EXPERTS_PER_GROUP + b)
    return np.asarray(lo, np.int32), np.asarray(hi, np.int32)


_CLASS_LO, _CLASS_HI = _class_tables()


def _pick_tile(n, candidates):
    for c in candidates:
        if n % c == 0:
            return c
    raise ValueError(f"no tile in {candidates} divides {n}")


def _params(*semantics):
    return pltpu.CompilerParams(dimension_semantics=semantics, vmem_limit_bytes=VMEM_LIMIT)


def _rms(x, g):
    return x * lax.rsqrt(jnp.mean(x * x, axis=-1, keepdims=True) + RMS_EPS) * g


def _qkv_kernel(h_ref, g_ref, w_ref, q_ref, k_ref, v_ref):
    _qkv_body(h_ref[...], g_ref, w_ref, q_ref, k_ref, v_ref)


def _qkv_body(h, g_ref, w_ref, q_ref, k_ref, v_ref):
    xn = _rms(h, g_ref[...]).astype(_BF16)
    scale = HEAD_DIM ** -0.5 * LOG2_E
    q = jnp.dot(xn, w_ref[:, 0:D_MODEL], preferred_element_type=_F32)
    q_ref[...] = (q * scale).astype(q_ref.dtype)
    k = jnp.dot(xn, w_ref[:, D_MODEL:2 * D_MODEL], preferred_element_type=_F32)
    k_ref[...] = k.astype(k_ref.dtype)
    v = jnp.dot(xn, w_ref[:, 2 * D_MODEL:3 * D_MODEL], preferred_element_type=_F32)
    v_ref[...] = v.astype(v_ref.dtype)


def _qkv(h2, gain, w_qkv):
    t = h2.shape[0]
    tm = _pick_tile(t, (512, 256))
    row = pl.BlockSpec((tm, D_MODEL), lambda i: (i, 0))
    out = jax.ShapeDtypeStruct((t, D_MODEL), _BF16)
    return pl.pallas_call(
        _qkv_kernel,
        grid=(t // tm,),
        in_specs=[row, pl.BlockSpec((1, D_MODEL), lambda i: (0, 0)),
                  pl.BlockSpec((D_MODEL, 3 * D_MODEL), lambda i: (0, 0))],
        out_specs=[row, row, row],
        out_shape=[out, out, out],
        compiler_params=_params("parallel"),
        name="qkv_proj",
    )(h2, gain, w_qkv)


def _neg_abs(x):
    bits = lax.bitcast_convert_type(x, jnp.uint32) | jnp.uint32(0x80000000)
    return lax.bitcast_convert_type(bits, _F32)


def _attn_kernel(q_ref, k_ref, v_ref, o_ref, acc_ref, carry_ref):
    qi = pl.program_id(2)
    tri_r = lax.broadcasted_iota(jnp.int32, (ATTN_TILE, ATTN_TILE), 0)
    tri_c = lax.broadcasted_iota(jnp.int32, (ATTN_TILE, ATTN_TILE), 1)
    neg_suffix = jnp.where(tri_r > tri_c, -1.0, 0.0).astype(_BF16)
    acc_ref[...] = jnp.zeros_like(acc_ref)
    carry_ref[...] = jnp.zeros_like(carry_ref)

    def step(kb, masked):
        start = pl.multiple_of(kb * ATTN_TILE, ATTN_TILE)
        if masked:
            q_pos = qi * ATTN_TILE + tri_r
            k_pos = kb * ATTN_TILE + tri_c
            mask = (k_pos < q_pos) & (k_pos >= FRONT)
        for g in range(HEADS_PER_STEP):
            cols = slice(g * HEAD_DIM, (g + 1) * HEAD_DIM)
            k = k_ref[0, pl.ds(start, ATTN_TILE), cols]
            v = v_ref[0, pl.ds(start, ATTN_TILE), cols]
            z = lax.dot_general(q_ref[0, :, cols], k, (((1,), (1,)), ((), ())), preferred_element_type=_F32)
            sp = jnp.maximum(z, 0.0) + jnp.log(1.0 + jnp.exp2(_neg_abs(z))) * LOG2_E
            if masked:
                sp = jnp.where(mask, sp, 0.0)
            within = jnp.dot(sp.astype(_BF16), neg_suffix, preferred_element_type=_F32)
            later = within + carry_ref[g]
            a = jnp.exp2(z - sp + later)
            if masked:
                a = jnp.where(mask, a, 0.0)
            acc_ref[g] += jnp.dot(a.astype(_BF16), v, preferred_element_type=_F32)
            carry_ref[g] = later[:, 0:1] - sp[:, 0:1]

    def alive():
        top = carry_ref[0]
        for g in range(1, HEADS_PER_STEP):
            top = jnp.maximum(top, carry_ref[g])
        return (jnp.max(top) > EXHAUSTED_LOG2).astype(jnp.int32)

    step(qi, True)

    def cond(state):
        kb, live = state
        return jnp.logical_and(kb >= 1, live > 0)

    def body(state):
        kb, _ = state
        step(kb, False)
        return kb - 1, alive()

    kb_end, live = lax.while_loop(cond, body, (qi - 1, alive()))

    @pl.when(jnp.logical_and(kb_end == 0, live > 0))
    def _():
        step(0, True)

    for g in range(HEADS_PER_STEP):
        o_ref[0, :, g * HEAD_DIM:(g + 1) * HEAD_DIM] = acc_ref[g].astype(o_ref.dtype)


def _attention(q, k, v):
    b, p, _ = q.shape
    nq = p // ATTN_TILE
    width = HEADS_PER_STEP * HEAD_DIM
    qspec = pl.BlockSpec((1, ATTN_TILE, width), lambda bi, hi, qi: (bi, qi, hi))
    kvspec = pl.BlockSpec((1, p, width), lambda bi, hi, qi: (bi, 0, hi), pipeline_mode=pl.Buffered(1))
    return pl.pallas_call(
        _attn_kernel,
        grid=(b, HEADS // HEADS_PER_STEP, nq),
        in_specs=[qspec, kvspec, kvspec],
        out_specs=qspec,
        out_shape=jax.ShapeDtypeStruct(q.shape, _BF16),
        scratch_shapes=[pltpu.VMEM((HEADS_PER_STEP, ATTN_TILE, HEAD_DIM), _F32),
                        pltpu.VMEM((HEADS_PER_STEP, ATTN_TILE, 1), _F32)],
        compiler_params=_params("parallel", "parallel", "arbitrary"),
        name="sb_attention",
    )(q, k, v)


def _route(h_new, g_ref, wr_ref, br_ref, cnt_ref, xg_ref, counts_ref):
    tm = h_new.shape[0]
    xn = _rms(h_new, g_ref[...])
    x_hi = xn.astype(_BF16)
    x_lo = (xn - x_hi.astype(_F32)).astype(_BF16)
    both = jnp.dot(x_hi, wr_ref[...], preferred_element_type=_F32)
    logits = (both[:, 0:LANES] + both[:, LANES:2 * LANES]
              + jnp.dot(x_lo, wr_ref[:, 0:LANES], preferred_element_type=_F32)) + br_ref[...]
    lane = lax.broadcasted_iota(jnp.int32, (tm, LANES), 1).astype(_F32)

    def first_argmax(vals, vmax):
        return jnp.min(jnp.where(vals >= vmax, lane, float(LANES)), axis=-1, keepdims=True)

    gl = jnp.where(lane < MOE_GROUPS, logits, NEG_BIG)
    gmax = jnp.max(gl, axis=-1, keepdims=True)
    gsel = first_argmax(gl, gmax)
    gprob = 1.0 / jnp.sum(jnp.exp(gl - gmax), axis=-1, keepdims=True)
    first = MOE_GROUPS + EXPERTS_PER_GROUP * gsel
    el = jnp.where((lane >= first) & (lane < first + EXPERTS_PER_GROUP), logits, NEG_BIG)
    m1 = jnp.max(el, axis=-1, keepdims=True)
    i1 = first_argmax(el, m1)
    el2 = jnp.where(lane == i1, NEG_BIG, el)
    m2 = jnp.max(el2, axis=-1, keepdims=True)
    i2 = first_argmax(el2, m2)
    ratio = jnp.exp(m2 - m1)
    gate1 = gprob / (1.0 + ratio)
    gate2 = gprob * ratio / (1.0 + ratio)
    a = jnp.minimum(i1, i2) - first
    bb = jnp.maximum(i1, i2) - first
    pair = (EXPERTS_PER_GROUP - 1) * a - a * (a - 1.0) * 0.5 + (bb - a - 1.0)
    cls = gsel * PAIRS_PER_GROUP + pair
    gate_lo = jnp.where(i1 < i2, gate1, gate2)
    gate_hi = jnp.where(i1 < i2, gate2, gate1)

    onehot = (lane == cls).astype(_F32)
    tri_r = lax.broadcasted_iota(jnp.int32, (tm, tm), 0)
    tri_c = lax.broadcasted_iota(jnp.int32, (tm, tm), 1)
    before = (tri_c < tri_r).astype(_BF16)
    seen = jnp.dot(before, onehot.astype(_BF16), preferred_element_type=_F32) + cnt_ref[...]
    rank = jnp.sum(onehot * seen, axis=-1, keepdims=True)
    cnt_ref[...] += jnp.sum(onehot, axis=0, keepdims=True)
    counts_ref[...] = cnt_ref[...]

    extra = jnp.where(lane == GATE_LO_LANE, gate_lo,
                      jnp.where(lane == GATE_HI_LANE, gate_hi,
                                jnp.where(lane == CLASS_LANE, cls,
                                          jnp.where(lane == RANK_LANE, rank, 0.0))))
    xg_ref[:, 0:D_MODEL] = xn
    xg_ref[:, D_MODEL:XG_WIDTH] = extra


def _router_weights(wg, bg, we, be):
    w = jnp.zeros((D_MODEL, LANES), _F32)
    w = w.at[:, 0:MOE_GROUPS].set(wg).at[:, MOE_GROUPS:MOE_GROUPS + N_EXPERTS].set(we)
    bias = jnp.zeros((1, LANES), _F32)
    bias = bias.at[0, 0:MOE_GROUPS].set(bg).at[0, MOE_GROUPS:MOE_GROUPS + N_EXPERTS].set(be)
    w_hi = w.astype(_BF16)
    w_lo = (w - w_hi.astype(_F32)).astype(_BF16)
    return jnp.concatenate([w_hi, w_lo], axis=1), bias


def _post_attn_kernel(o_ref, h_ref, wo_ref, g_ref, wr_ref, br_ref,
                      h_out_ref, xg_ref, counts_ref, cnt_ref):
    @pl.when(pl.program_id(0) == 0)
    def _():
        cnt_ref[...] = jnp.zeros_like(cnt_ref)

    h_new = h_ref[...] + jnp.dot(o_ref[...], wo_ref[...], preferred_element_type=_F32)
    h_out_ref[...] = h_new
    _route(h_new, g_ref, wr_ref, br_ref, cnt_ref, xg_ref, counts_ref)


def _const_spec(shape):
    return pl.BlockSpec(shape, lambda *_: (0,) * len(shape))


def _post_attn(o2, h2, w_o, gain, wr, br):
    t = h2.shape[0]
    tm = _pick_tile(t, (512, 256))
    row = pl.BlockSpec((tm, D_MODEL), lambda i: (i, 0))
    return pl.pallas_call(
        _post_attn_kernel,
        grid=(t // tm,),
        in_specs=[row, row, _const_spec((D_MODEL, D_MODEL)), _const_spec((1, D_MODEL)),
                  _const_spec((D_MODEL, 2 * LANES)), _const_spec((1, LANES))],
        out_specs=[row, pl.BlockSpec((tm, XG_WIDTH), lambda i: (i, 0)), _const_spec((1, LANES))],
        out_shape=[jax.ShapeDtypeStruct((t, D_MODEL), _F32), jax.ShapeDtypeStruct((t, XG_WIDTH), _F32),
                   jax.ShapeDtypeStruct((1, LANES), _F32)],
        scratch_shapes=[pltpu.VMEM((1, LANES), _F32)],
        compiler_params=_params("arbitrary"),
        name="attn_out_route",
    )(o2, h2, w_o, gain, wr, br)


def _pool_kernel(dest_ref, dest_next_ref, y_ref, h_ref, mg_ref, pw_ref, ps_ref, g_ref, wr_ref, br_ref,
                 h_out_ref, xg_ref, counts_ref, rows_ref, sems, cnt_ref, tail_ref, *, tile, tiles_per_seq):
    step = pl.program_id(0)
    i = step % tiles_per_seq

    @pl.when(step == 0)
    def _():
        cnt_ref[...] = jnp.zeros_like(cnt_ref)
        tail_ref[...] = jnp.zeros_like(tail_ref)

    rows, fetch_next = _expert_rows(dest_ref, dest_next_ref, y_ref, rows_ref, sems, tile)
    h = h_ref[...] + rows
    fetch_next()
    hn = _rms(h, mg_ref[...])
    halo = jnp.where(i == 0, 0.0, tail_ref[...])
    tail_ref[...] = hn[tile - POOL_HALO:, :]
    cat = jnp.concatenate([halo, hn], axis=0)
    token = i * tile + lax.broadcasted_iota(jnp.int32, (tile, 1), 0) - FRONT
    outs = []
    for gi, w in enumerate(POOL_WINDOWS):
        cols = slice(gi * POOL_GROUP_DIM, (gi + 1) * POOL_GROUP_DIM)
        xg = hn[:, cols]
        run = cat[:, cols]
        shift = 1
        while shift < w:
            run = run + pltpu.roll(run, shift, axis=0)
            shift *= 2
        wsum = run[POOL_HALO:, :]
        count = jnp.clip(token + 1, 1, w).astype(_F32)
        mixed = wsum / count - xg
        outs.append(jnp.dot(mixed.astype(_BF16), pw_ref[gi], preferred_element_type=_F32))
    h_new = h + jnp.concatenate(outs, axis=1) * ps_ref[...]
    h_out_ref[...] = h_new
    _route(h_new, g_ref, wr_ref, br_ref, cnt_ref, xg_ref, counts_ref)
    _expert_rows_finish(dest_ref, y_ref, rows_ref, sems, tile)


def _pool(y_sorted, dest, h2, seq_rows, mix_gain, pool_w, pool_scale, gain, wr, br):
    t = h2.shape[0]
    tile = _pick_tile(seq_rows, (768, 256))
    n = t // tile
    dest3 = dest.reshape(n, 1, tile)
    row = pl.BlockSpec((tile, D_MODEL), lambda s: (s, 0))
    return pl.pallas_call(
        functools.partial(_pool_kernel, tile=tile, tiles_per_seq=seq_rows // tile),
        grid=(n,),
        in_specs=_expert_rows_specs(n, tile, lambda s: s)
        + [row, _const_spec((1, D_MODEL)),
           _const_spec((len(POOL_WINDOWS), POOL_GROUP_DIM, POOL_GROUP_DIM)), _const_spec((1, D_MODEL)),
           _const_spec((1, D_MODEL)), _const_spec((D_MODEL, 2 * LANES)), _const_spec((1, LANES))],
        out_specs=[row, pl.BlockSpec((tile, XG_WIDTH), lambda s: (s, 0)), _const_spec((1, LANES))],
        out_shape=[jax.ShapeDtypeStruct(h2.shape, _F32), jax.ShapeDtypeStruct((t, XG_WIDTH), _F32),
                   jax.ShapeDtypeStruct((1, LANES), _F32)],
        scratch_shapes=_expert_rows_scratch(tile)
        + [pltpu.VMEM((1, LANES), _F32), pltpu.VMEM((POOL_HALO, D_MODEL), _F32)],
        input_output_aliases={3: 0},
        compiler_params=_params("arbitrary"),
        name="combine_pool_route",
    )(dest3, dest3, y_sorted, h2, mix_gain, pool_w, pool_scale, gain, wr, br)


def _issue_rows(tm, copy_for_row):
    def group(g, c):
        r0 = pl.multiple_of(g * SUBLANES, SUBLANES)
        for j in range(SUBLANES):
            copy_for_row(r0, j).start()
        return c

    lax.fori_loop(0, tm // SUBLANES, group, 0)


def _issue_rows_inline(tm, copy_for_row):
    for r0 in range(0, tm, SUBLANES):
        for j in range(SUBLANES):
            copy_for_row(r0, j).start()


def _tile_row(ref, r0, j):
    return ref.at[pl.ds(r0, SUBLANES)].at[pl.ds(j, 1)]


def _wait_rows(tm, copy):
    def wait(r, c):
        copy.wait()
        return c

    lax.fori_loop(0, tm, wait, 0, unroll=8)


def _dispatch_kernel(dest_ref, xg_ref, out_ref, sem, *, tm):
    def copy(r0, j):
        return pltpu.make_async_copy(_tile_row(xg_ref, r0, j), out_ref.at[pl.ds(dest_ref[0, 0, r0 + j], 1)], sem)

    _issue_rows_inline(tm, copy)
    _wait_rows(tm, copy(0, 0))


def _dispatch(xg, dest3):
    t = xg.shape[0]
    tm = dest3.shape[-1]
    return pl.pallas_call(
        functools.partial(_dispatch_kernel, tm=tm),
        grid=(t // tm,),
        in_specs=[pl.BlockSpec((1, 1, tm), lambda i: (i, 0, 0), memory_space=pltpu.SMEM),
                  pl.BlockSpec((tm, XG_WIDTH), lambda i: (i, 0))],
        out_specs=pl.BlockSpec(memory_space=pl.ANY),
        out_shape=jax.ShapeDtypeStruct(xg.shape, _F32),
        scratch_shapes=[pltpu.SemaphoreType.DMA(())],
        compiler_params=_params("arbitrary"),
        name="moe_dispatch",
    )(dest3, xg)


def _expert_kernel(blk_ref, lo_ref, hi_ref, elo_ref, ehi_ref,
                   x_ref, wg_lo, wu_lo, wd_lo, wg_hi, wu_hi, wd_hi, y_ref):
    del elo_ref, ehi_ref
    j = pl.program_id(0)

    @pl.when(jnp.logical_or(j == 0, blk_ref[jnp.maximum(j - 1, 0)] != blk_ref[j]))
    def _():
        y_ref[...] = jnp.zeros_like(y_ref)

    lo = lo_ref[j]
    hi = hi_ref[j]

    @pl.when(hi > lo)
    def _():
        row = lax.broadcasted_iota(jnp.int32, (ROW_BLOCK, 1), 0)
        inside = (row >= lo) & (row < hi)
        xb = x_ref[:, 0:D_MODEL].astype(_BF16)

        def expert(wg, wu, wd):
            gate = jnp.dot(xb, wg[0], preferred_element_type=_F32)
            up = jnp.dot(xb, wu[0], preferred_element_type=_F32)
            hid = gate * jax.nn.sigmoid(gate) * up
            return jnp.dot(hid.astype(_BF16), wd[0], preferred_element_type=_F32)

        g_lo = jnp.where(inside, x_ref[:, D_MODEL + GATE_LO_LANE:D_MODEL + GATE_LO_LANE + 1], 0.0)
        g_hi = jnp.where(inside, x_ref[:, D_MODEL + GATE_HI_LANE:D_MODEL + GATE_HI_LANE + 1], 0.0)
        y_ref[...] += expert(wg_lo, wu_lo, wd_lo) * g_lo + expert(wg_hi, wu_hi, wd_hi) * g_hi


def _experts(x_sorted, item_blk, item_lo, item_hi, item_elo, item_ehi, w_gate, w_up, w_down):
    n_items = item_blk.shape[0]

    def w_spec(shape, which):
        def index(j, blk, lo, hi, elo, ehi):
            return ((elo, ehi)[which][j], 0, 0)
        return pl.BlockSpec(shape, index)

    w_in = (1, D_MODEL, EXPERT_FF)
    w_out = (1, EXPERT_FF, D_MODEL)
    grid_spec = pltpu.PrefetchScalarGridSpec(
        num_scalar_prefetch=5,
        grid=(n_items,),
        in_specs=[pl.BlockSpec((ROW_BLOCK, XG_WIDTH), lambda j, blk, *_: (blk[j], 0)),
                  w_spec(w_in, 0), w_spec(w_in, 0), w_spec(w_out, 0),
                  w_spec(w_in, 1), w_spec(w_in, 1), w_spec(w_out, 1)],
        out_specs=pl.BlockSpec((ROW_BLOCK, D_MODEL), lambda j, blk, *_: (blk[j], 0)),
    )
    return pl.pallas_call(
        _expert_kernel,
        grid_spec=grid_spec,
        out_shape=jax.ShapeDtypeStruct((x_sorted.shape[0], D_MODEL), _F32),
        compiler_params=_params("arbitrary"),
        name="moe_experts",
    )(item_blk, item_lo, item_hi, item_elo, item_ehi, x_sorted, w_gate, w_up, w_down, w_gate, w_up, w_down)


def _expert_rows(dest_ref, dest_next_ref, y_ref, rows_ref, sems, tm):
    i = pl.program_id(0)
    slot = i % 2

    def row_copy(d_ref, s):
        def copy(r0, j):
            src = y_ref.at[pl.ds(d_ref[0, 0, r0 + j], 1)]
            return pltpu.make_async_copy(src, _tile_row(rows_ref.at[s], r0, j), sems.at[s])
        return copy

    @pl.when(i == 0)
    def _():
        _issue_rows(tm, row_copy(dest_ref, 0))

    _wait_rows(tm, row_copy(dest_ref, slot)(0, 0))
    return rows_ref[slot], lambda: _issue_rows_inline(tm, row_copy(dest_next_ref, 1 - slot))


def _expert_rows_finish(dest_ref, y_ref, rows_ref, sems, tm):
    i = pl.program_id(0)

    @pl.when(i == pl.num_programs(0) - 1)
    def _():
        other = 1 - i % 2
        src = y_ref.at[pl.ds(dest_ref[0, 0, 0], 1)]
        _wait_rows(tm, pltpu.make_async_copy(src, _tile_row(rows_ref.at[other], 0, 0), sems.at[other]))


def _expert_rows_specs(n_steps, tm, tile_of_step):
    def dest(offset):
        return pl.BlockSpec((1, 1, tm), lambda i: (tile_of_step(jnp.minimum(i + offset, n_steps - 1)), 0, 0),
                            memory_space=pltpu.SMEM)
    return [dest(0), dest(1), pl.BlockSpec(memory_space=pl.ANY)]


def _expert_rows_scratch(tm):
    return [pltpu.VMEM((2, tm, D_MODEL), _F32), pltpu.SemaphoreType.DMA((2,))]


def _qkv_combine_kernel(dest_ref, dest_next_ref, y_ref, h_ref, g_ref, w_ref,
                        h_out_ref, q_ref, k_ref, v_ref, rows_ref, sems, *, tm):
    rows, fetch_next = _expert_rows(dest_ref, dest_next_ref, y_ref, rows_ref, sems, tm)
    h = h_ref[...] + rows
    fetch_next()
    h_out_ref[...] = h
    _qkv_body(h, g_ref, w_ref, q_ref, k_ref, v_ref)
    _expert_rows_finish(dest_ref, y_ref, rows_ref, sems, tm)


def _qkv_combine(y_sorted, dest, h2, gain, w_qkv):
    t = h2.shape[0]
    tm = _pick_tile(t, (512, 256))
    n = t // tm
    dest3 = dest.reshape(n, 1, tm)
    row = pl.BlockSpec((tm, D_MODEL), lambda i: (i, 0))
    out = jax.ShapeDtypeStruct((t, D_MODEL), _BF16)
    return pl.pallas_call(
        functools.partial(_qkv_combine_kernel, tm=tm),
        grid=(n,),
        in_specs=_expert_rows_specs(n, tm, lambda s: s)
        + [row, _const_spec((1, D_MODEL)), _const_spec((D_MODEL, 3 * D_MODEL))],
        out_specs=[row, row, row, row],
        out_shape=[jax.ShapeDtypeStruct((t, D_MODEL), _F32), out, out, out],
        scratch_shapes=_expert_rows_scratch(tm),
        input_output_aliases={3: 0},
        compiler_params=_params("arbitrary"),
        name="combine_qkv_proj",
    )(dest3, dest3, y_sorted, h2, gain, w_qkv)


def _moe(xg, counts, w_gate, w_up, w_down):
    t = xg.shape[0]
    tm = _pick_tile(t, (1024, 512, 256))
    n_blocks = t // ROW_BLOCK
    cnt = counts[0, :N_CLASSES].astype(jnp.int32)
    ends = jnp.cumsum(cnt)
    starts = ends - cnt
    cuts = jnp.sort(jnp.concatenate([jnp.arange(n_blocks, dtype=jnp.int32) * ROW_BLOCK, starts[1:]]))
    item_blk = jnp.minimum(cuts // ROW_BLOCK, n_blocks - 1)
    item_lo = cuts - item_blk * ROW_BLOCK
    item_hi = jnp.concatenate([cuts[1:], jnp.full((1,), t, jnp.int32)]) - item_blk * ROW_BLOCK
    item_cls = jnp.minimum(jnp.searchsorted(ends, cuts, side="right"), N_CLASSES - 1)
    item_elo = jnp.asarray(_CLASS_LO)[item_cls]
    item_ehi = jnp.asarray(_CLASS_HI)[item_cls]
    cls = xg[:, D_MODEL + CLASS_LANE].astype(jnp.int32)
    rank = xg[:, D_MODEL + RANK_LANE].astype(jnp.int32)
    class_ids = jnp.arange(N_CLASSES, dtype=jnp.int32)
    start_of = jnp.sum(jnp.where(cls[:, None] == class_ids[None, :], starts[None, :], 0), axis=1)
    dest = (start_of + rank).astype(jnp.int32)
    x_sorted = _dispatch(xg, dest.reshape(t // tm, 1, tm))
    y_sorted = _experts(x_sorted, item_blk, item_lo, item_hi, item_elo, item_ehi, w_gate, w_up, w_down)
    return y_sorted, dest


def _final_kernel(dest_ref, dest_next_ref, y_ref, h_ref, g_ref, o_ref, rows_ref, sems, *, tm):
    rows, fetch_next = _expert_rows(dest_ref, dest_next_ref, y_ref, rows_ref, sems, tm)
    h = h_ref[...] + rows
    fetch_next()
    o_ref[...] = _rms(h, g_ref[...])
    _expert_rows_finish(dest_ref, y_ref, rows_ref, sems, tm)


def _final(y_sorted, dest, h2, gain, b, seq):
    tm = ATTN_TILE
    per_batch = seq // tm
    n = b * per_batch

    def tile_of_step(s):
        return (s // per_batch) * (per_batch + 1) + s % per_batch + 1

    out = pl.pallas_call(
        functools.partial(_final_kernel, tm=tm),
        grid=(n,),
        in_specs=_expert_rows_specs(n, tm, tile_of_step)
        + [pl.BlockSpec((tm, D_MODEL), lambda i: (tile_of_step(i), 0)), _const_spec((1, D_MODEL))],
        out_specs=pl.BlockSpec((tm, D_MODEL), lambda i: (i, 0)),
        out_shape=jax.ShapeDtypeStruct((b * seq, D_MODEL), _F32),
        scratch_shapes=_expert_rows_scratch(tm),
        compiler_params=_params("arbitrary"),
        name="final_norm",
    )(dest.reshape(-1, 1, tm), dest.reshape(-1, 1, tm), y_sorted, h2, gain)
    return out.reshape(b, seq, D_MODEL)


def kernel(x, meta_tokens, mix_norm, ffn_norm, final_norm, sb_w_qkv, sb_w_o, pool_w, pool_scale,
           router_group_w, router_group_b, router_expert_w, router_expert_b,
           expert_w_gate, expert_w_up, expert_w_down):
    b, seq, d = x.shape
    assert d == D_MODEL and seq % ATTN_TILE == 0
    depth = mix_norm.shape[0]
    p = seq + ATTN_TILE
    t = b * p
    meta = jnp.broadcast_to(meta_tokens[None].astype(x.dtype), (b, N_META, d))
    h2 = jnp.concatenate([jnp.zeros((b, FRONT, d), x.dtype), meta, x], axis=1).reshape(t, d)
    moe_out = None

    for i in range(depth):
        j = i // 2
        wr, br = _router_weights(router_group_w[i], router_group_b[i], router_expert_w[i], router_expert_b[i])
        ffn_gain = ffn_norm[i].reshape(1, d)
        mix_gain = mix_norm[i].reshape(1, d)
        if i % 2 == 0:
            w_qkv = sb_w_qkv[j].astype(_BF16)
            if moe_out is None:
                q, k, v = _qkv(h2, mix_gain, w_qkv)
            else:
                h2, q, k, v = _qkv_combine(*moe_out, h2, mix_gain, w_qkv)
            o = _attention(q.reshape(b, p, d), k.reshape(b, p, d), v.reshape(b, p, d))
            h2, xg, counts = _post_attn(o.reshape(t, d), h2, sb_w_o[j].astype(_BF16), ffn_gain, wr, br)
        else:
            assert moe_out is not None
            h2, xg, counts = _pool(*moe_out, h2, p, mix_gain, pool_w[j].astype(_BF16),
                                   pool_scale[j].reshape(1, d), ffn_gain, wr, br)
        moe_out = _moe(xg, counts, expert_w_gate[i].astype(_BF16), expert_w_up[i].astype(_BF16),
                       expert_w_down[i].astype(_BF16))
    return _final(*moe_out, h2, final_norm.reshape(1, d), b, seq)
```

```python
import functools

import numpy as np
import jax
import jax.numpy as jnp
from jax import lax
from jax.experimental import pallas as pl
from jax.experimental.pallas import tpu as pltpu

D_MODEL = 1024
N_META = 16
HEADS = 8
HEAD_DIM = D_MODEL // HEADS
POOL_WINDOWS = (2, 4, 8, 16)
POOL_GROUP_DIM = D_MODEL // len(POOL_WINDOWS)
POOL_HALO = 16
MOE_GROUPS = 4
EXPERTS_PER_GROUP = 8
N_EXPERTS = MOE_GROUPS * EXPERTS_PER_GROUP
EXPERT_FF = D_MODEL // 4
PAIRS_PER_GROUP = EXPERTS_PER_GROUP * (EXPERTS_PER_GROUP - 1) // 2
N_CLASSES = MOE_GROUPS * PAIRS_PER_GROUP
RMS_EPS = 1e-6

LANES = 128
SUBLANES = 8
DMA_THREADS = 2
ATTN_TILE = 256
FRONT = ATTN_TILE - N_META
HEADS_PER_STEP = 8
LOG2_E = 1.4426950408889634
EXHAUSTED_LOG2 = -200.0
ROW_BLOCK = 256
XG_WIDTH = D_MODEL + LANES
GATE_LO_LANE, GATE_HI_LANE, CLASS_LANE, RANK_LANE = 0, 1, 2, 3
NEG_BIG = -1e30
VMEM_LIMIT = 48 * 1024 * 1024

_F32 = jnp.float32
_BF16 = jnp.bfloat16


def _class_tables():
    lo, hi = [], []
    for g in range(MOE_GROUPS):
        for a in range(EXPERTS_PER_GROUP):
            for b in range(a + 1, EXPERTS_PER_GROUP):
                lo.append(g * EXPERTS_PER_GROUP + a)
                hi.append(g * EXPERTS_PER_GROUP + b)
    return np.asarray(lo, np.int32), np.asarray(hi, np.int32)


_CLASS_LO, _CLASS_HI = _class_tables()


def _pick_tile(n, candidates):
    for c in candidates:
        if n % c == 0:
            return c
    raise ValueError(f"no tile in {candidates} divides {n}")


def _params(*semantics):
    return pltpu.CompilerParams(dimension_semantics=semantics, vmem_limit_bytes=VMEM_LIMIT)


def _rms(x, g):
    return x * lax.rsqrt(jnp.mean(x * x, axis=-1, keepdims=True) + RMS_EPS) * g


def _qkv_kernel(h_ref, g_ref, w_ref, q_ref, k_ref, v_ref):
    _qkv_body(h_ref[...], g_ref, w_ref, q_ref, k_ref, v_ref)


def _qkv_body(h, g_ref, w_ref, q_ref, k_ref, v_ref):
    xn = _rms(h, g_ref[...]).astype(_BF16)
    scale = HEAD_DIM ** -0.5 * LOG2_E
    q = jnp.dot(xn, w_ref[:, 0:D_MODEL], preferred_element_type=_F32)
    q_ref[...] = (q * scale).astype(q_ref.dtype)
    k = jnp.dot(xn, w_ref[:, D_MODEL:2 * D_MODEL], preferred_element_type=_F32)
    k_ref[...] = k.astype(k_ref.dtype)
    v = jnp.dot(xn, w_ref[:, 2 * D_MODEL:3 * D_MODEL], preferred_element_type=_F32)
    v_ref[...] = v.astype(v_ref.dtype)


def _qkv(h2, gain, w_qkv):
    t = h2.shape[0]
    tm = _pick_tile(t, (512, 256))
    row = pl.BlockSpec((tm, D_MODEL), lambda i: (i, 0))
    out = jax.ShapeDtypeStruct((t, D_MODEL), _BF16)
    return pl.pallas_call(
        _qkv_kernel,
        grid=(t // tm,),
        in_specs=[row, pl.BlockSpec((1, D_MODEL), lambda i: (0, 0)),
                  pl.BlockSpec((D_MODEL, 3 * D_MODEL), lambda i: (0, 0))],
        out_specs=[row, row, row],
        out_shape=[out, out, out],
        compiler_params=_params("parallel"),
        name="qkv_proj",
    )(h2, gain, w_qkv)


def _neg_abs(x):
    bits = lax.bitcast_convert_type(x, jnp.uint32) | jnp.uint32(0x80000000)
    return lax.bitcast_convert_type(bits, _F32)


def _attn_kernel(q_ref, k_ref, v_ref, o_ref, acc_ref, carry_ref):
    qi = pl.program_id(2)
    tri_r = lax.broadcasted_iota(jnp.int32, (ATTN_TILE, ATTN_TILE), 0)
    tri_c = lax.broadcasted_iota(jnp.int32, (ATTN_TILE, ATTN_TILE), 1)
    neg_suffix = jnp.where(tri_r > tri_c, -1.0, 0.0).astype(_BF16)
    acc_ref[...] = jnp.zeros_like(acc_ref)
    carry_ref[...] = jnp.zeros_like(carry_ref)

    def step(kb, masked):
        start = pl.multiple_of(kb * ATTN_TILE, ATTN_TILE)
        if masked:
            q_pos = qi * ATTN_TILE + tri_r
            k_pos = kb * ATTN_TILE + tri_c
            mask = (k_pos < q_pos) & (k_pos >= FRONT)
        for g in range(HEADS_PER_STEP):
            cols = slice(g * HEAD_DIM, (g + 1) * HEAD_DIM)
            k = k_ref[0, pl.ds(start, ATTN_TILE), cols]
            v = v_ref[0, pl.ds(start, ATTN_TILE), cols]
            z = lax.dot_general(q_ref[0, :, cols], k, (((1,), (1,)), ((), ())), preferred_element_type=_F32)
            sp = jnp.maximum(z, 0.0) + jnp.log(1.0 + jnp.exp2(_neg_abs(z))) * LOG2_E
            if masked:
                sp = jnp.where(mask, sp, 0.0)
            within = jnp.dot(sp.astype(_BF16), neg_suffix, preferred_element_type=_F32)
            later = within + carry_ref[g]
            a = jnp.exp2(z - sp + later)
            if masked:
                a = jnp.where(mask, a, 0.0)
            acc_ref[g] += jnp.dot(a.astype(_BF16), v, preferred_element_type=_F32)
            carry_ref[g] = later[:, 0:1] - sp[:, 0:1]

    def alive():
        top = carry_ref[0]
        for g in range(1, HEADS_PER_STEP):
            top = jnp.maximum(top, carry_ref[g])
        return (jnp.max(top) > EXHAUSTED_LOG2).astype(jnp.int32)

    step(qi, True)

    def cond(state):
        kb, live = state
        return jnp.logical_and(kb >= 1, live > 0)

    def body(state):
        kb, _ = state
        step(kb, False)
        return kb - 1, alive()

    kb_end, live = lax.while_loop(cond, body, (qi - 1, alive()))

    @pl.when(jnp.logical_and(kb_end == 0, live > 0))
    def _():
        step(0, True)

    for g in range(HEADS_PER_STEP):
        o_ref[0, :, g * HEAD_DIM:(g + 1) * HEAD_DIM] = acc_ref[g].astype(o_ref.dtype)


def _attention(q, k, v):
    b, p, _ = q.shape
    nq = p // ATTN_TILE
    width = HEADS_PER_STEP * HEAD_DIM
    qspec = pl.BlockSpec((1, ATTN_TILE, width), lambda bi, hi, qi: (bi, qi, hi))
    kvspec = pl.BlockSpec((1, p, width), lambda bi, hi, qi: (bi, 0, hi), pipeline_mode=pl.Buffered(1))
    return pl.pallas_call(
        _attn_kernel,
        grid=(b, HEADS // HEADS_PER_STEP, nq),
        in_specs=[qspec, kvspec, kvspec],
        out_specs=qspec,
        out_shape=jax.ShapeDtypeStruct(q.shape, _BF16),
        scratch_shapes=[pltpu.VMEM((HEADS_PER_STEP, ATTN_TILE, HEAD_DIM), _F32),
                        pltpu.VMEM((HEADS_PER_STEP, ATTN_TILE, 1), _F32)],
        compiler_params=_params("parallel", "parallel", "arbitrary"),
        name="sb_attention",
    )(q, k, v)


def _route(h_new, g_ref, wr_ref, br_ref, cnt_ref, xg_ref, counts_ref):
    tm = h_new.shape[0]
    xn = _rms(h_new, g_ref[...])
    x_hi = xn.astype(_BF16)
    x_lo = (xn - x_hi.astype(_F32)).astype(_BF16)
    both = jnp.dot(x_hi, wr_ref[...], preferred_element_type=_F32)
    logits = (both[:, 0:LANES] + both[:, LANES:2 * LANES]
              + jnp.dot(x_lo, wr_ref[:, 0:LANES], preferred_element_type=_F32)) + br_ref[...]
    lane = lax.broadcasted_iota(jnp.int32, (tm, LANES), 1).astype(_F32)

    def first_argmax(vals, vmax):
        return jnp.min(jnp.where(vals >= vmax, lane, float(LANES)), axis=-1, keepdims=True)

    gl = jnp.where(lane < MOE_GROUPS, logits, NEG_BIG)
    gmax = jnp.max(gl, axis=-1, keepdims=True)
    gsel = first_argmax(gl, gmax)
    gprob = 1.0 / jnp.sum(jnp.exp(gl - gmax), axis=-1, keepdims=True)
    first = MOE_GROUPS + EXPERTS_PER_GROUP * gsel
    el = jnp.where((lane >= first) & (lane < first + EXPERTS_PER_GROUP), logits, NEG_BIG)
    m1 = jnp.max(el, axis=-1, keepdims=True)
    i1 = first_argmax(el, m1)
    el2 = jnp.where(lane == i1, NEG_BIG, el)
    m2 = jnp.max(el2, axis=-1, keepdims=True)
    i2 = first_argmax(el2, m2)
    ratio = jnp.exp(m2 - m1)
    gate1 = gprob / (1.0 + ratio)
    gate2 = gprob * ratio / (1.0 + ratio)
    a = jnp.minimum(i1, i2) - first
    bb = jnp.maximum(i1, i2) - first
    pair = (EXPERTS_PER_GROUP - 1) * a - a * (a - 1.0) * 0.5 + (bb - a - 1.0)
    cls = gsel * PAIRS_PER_GROUP + pair
    gate_lo = jnp.where(i1 < i2, gate1, gate2)
    gate_hi = jnp.where(i1 < i2, gate2, gate1)

    onehot = (lane == cls).astype(_F32)
    tri_r = lax.broadcasted_iota(jnp.int32, (tm, tm), 0)
    tri_c = lax.broadcasted_iota(jnp.int32, (tm, tm), 1)
    before = (tri_c < tri_r).astype(_BF16)
    seen = jnp.dot(before, onehot.astype(_BF16), preferred_element_type=_F32) + cnt_ref[...]
    rank = jnp.sum(onehot * seen, axis=-1, keepdims=True)
    cnt_ref[...] += jnp.sum(onehot, axis=0, keepdims=True)
    counts_ref[...] = cnt_ref[...]

    extra = jnp.where(lane == GATE_LO_LANE, gate_lo,
                      jnp.where(lane == GATE_HI_LANE, gate_hi,
                                jnp.where(lane == CLASS_LANE, cls,
                                          jnp.where(lane == RANK_LANE, rank, 0.0))))
    xg_ref[:, 0:D_MODEL] = xn
    xg_ref[:, D_MODEL:XG_WIDTH] = extra


def _router_weights(wg, bg, we, be):
    w = jnp.zeros((D_MODEL, LANES), _F32)
    w = w.at[:, 0:MOE_GROUPS].set(wg).at[:, MOE_GROUPS:MOE_GROUPS + N_EXPERTS].set(we)
    bias = jnp.zeros((1, LANES), _F32)
    bias = bias.at[0, 0:MOE_GROUPS].set(bg).at[0, MOE_GROUPS:MOE_GROUPS + N_EXPERTS].set(be)
    w_hi = w.astype(_BF16)
    w_lo = (w - w_hi.astype(_F32)).astype(_BF16)
    return jnp.concatenate([w_hi, w_lo], axis=1), bias


def _post_attn_kernel(o_ref, h_ref, wo_ref, g_ref, wr_ref, br_ref,
                      h_out_ref, xg_ref, counts_ref, cnt_ref):
    @pl.when(pl.program_id(0) == 0)
    def _():
        cnt_ref[...] = jnp.zeros_like(cnt_ref)

    h_new = h_ref[...] + jnp.dot(o_ref[...], wo_ref[...], preferred_element_type=_F32)
    h_out_ref[...] = h_new
    _route(h_new, g_ref, wr_ref, br_ref, cnt_ref, xg_ref, counts_ref)


def _const_spec(shape):
    return pl.BlockSpec(shape, lambda *_: (0,) * len(shape))


def _post_attn(o2, h2, w_o, gain, wr, br):
    t = h2.shape[0]
    tm = _pick_tile(t, (512, 256))
    row = pl.BlockSpec((tm, D_MODEL), lambda i: (i, 0))
    return pl.pallas_call(
        _post_attn_kernel,
        grid=(t // tm,),
        in_specs=[row, row, _const_spec((D_MODEL, D_MODEL)), _const_spec((1, D_MODEL)),
                  _const_spec((D_MODEL, 2 * LANES)), _const_spec((1, LANES))],
        out_specs=[row, pl.BlockSpec((tm, XG_WIDTH), lambda i: (i, 0)), _const_spec((1, LANES))],
        out_shape=[jax.ShapeDtypeStruct((t, D_MODEL), _F32), jax.ShapeDtypeStruct((t, XG_WIDTH), _F32),
                   jax.ShapeDtypeStruct((1, LANES), _F32)],
        scratch_shapes=[pltpu.VMEM((1, LANES), _F32)],
        compiler_params=_params("arbitrary"),
        name="attn_out_route",
    )(o2, h2, w_o, gain, wr, br)


def _pool_kernel(dest_ref, dest_next_ref, y_ref, h_ref, mg_ref, pw_ref, ps_ref, g_ref, wr_ref, br_ref,
                 h_out_ref, xg_ref, counts_ref, rows_ref, sems, cnt_ref, tail_ref, *, tile, tiles_per_seq):
    step = pl.program_id(0)
    i = step % tiles_per_seq

    @pl.when(step == 0)
    def _():
        cnt_ref[...] = jnp.zeros_like(cnt_ref)
        tail_ref[...] = jnp.zeros_like(tail_ref)

    rows, fetch_next = _expert_rows(dest_ref, dest_next_ref, y_ref, rows_ref, sems, tile)
    h = h_ref[...] + rows
    fetch_next()
    hn = _rms(h, mg_ref[...])
    halo = jnp.where(i == 0, 0.0, tail_ref[...])
    tail_ref[...] = hn[tile - POOL_HALO:, :]
    cat = jnp.concatenate([halo, hn], axis=0)
    token = i * tile + lax.broadcasted_iota(jnp.int32, (tile, 1), 0) - FRONT
    outs = []
    for gi, w in enumerate(POOL_WINDOWS):
        cols = slice(gi * POOL_GROUP_DIM, (gi + 1) * POOL_GROUP_DIM)
        xg = hn[:, cols]
        run = cat[:, cols]
        shift = 1
        while shift < w:
            run = run + pltpu.roll(run, shift, axis=0)
            shift *= 2
        wsum = run[POOL_HALO:, :]
        count = jnp.clip(token + 1, 1, w).astype(_F32)
        mixed = wsum / count - xg
        outs.append(jnp.dot(mixed.astype(_BF16), pw_ref[gi], preferred_element_type=_F32))
    h_new = h + jnp.concatenate(outs, axis=1) * ps_ref[...]
    h_out_ref[...] = h_new
    _route(h_new, g_ref, wr_ref, br_ref, cnt_ref, xg_ref, counts_ref)
    _expert_rows_finish(dest_ref, y_ref, rows_ref, sems, tile)


def _pool(y_sorted, dest, h2, seq_rows, mix_gain, pool_w, pool_scale, gain, wr, br):
    t = h2.shape[0]
    tile = _pick_tile(seq_rows, (768, 256))
    n = t // tile
    dest3 = dest.reshape(n, 1, tile)
    row = pl.BlockSpec((tile, D_MODEL), lambda s: (s, 0))
    return pl.pallas_call(
        functools.partial(_pool_kernel, tile=tile, tiles_per_seq=seq_rows // tile),
        grid=(n,),
        in_specs=_expert_rows_specs(n, tile, lambda s: s)
        + [row, _const_spec((1, D_MODEL)),
           _const_spec((len(POOL_WINDOWS), POOL_GROUP_DIM, POOL_GROUP_DIM)), _const_spec((1, D_MODEL)),
           _const_spec((1, D_MODEL)), _const_spec((D_MODEL, 2 * LANES)), _const_spec((1, LANES))],
        out_specs=[row, pl.BlockSpec((tile, XG_WIDTH), lambda s: (s, 0)), _const_spec((1, LANES))],
        out_shape=[jax.ShapeDtypeStruct(h2.shape, _F32), jax.ShapeDtypeStruct((t, XG_WIDTH), _F32),
                   jax.ShapeDtypeStruct((1, LANES), _F32)],
        scratch_shapes=_expert_rows_scratch(tile)
        + [pltpu.VMEM((1, LANES), _F32), pltpu.VMEM((POOL_HALO, D_MODEL), _F32)],
        input_output_aliases={3: 0},
        compiler_params=_params("arbitrary"),
        name="combine_pool_route",
    )(dest3, dest3, y_sorted, h2, mix_gain, pool_w, pool_scale, gain, wr, br)


def _issue_rows(tm, copy_for_row):
    def group(g, c):
        r0 = pl.multiple_of(g * SUBLANES, SUBLANES)
        for j in range(SUBLANES):
            copy_for_row(r0, j).start(priority=j % DMA_THREADS)
        return c

    lax.fori_loop(0, tm // SUBLANES, group, 0)


def _issue_rows_inline(tm, copy_for_row):
    for r0 in range(0, tm, SUBLANES):
        for j in range(SUBLANES):
            copy_for_row(r0, j).start(priority=j % DMA_THREADS)


def _tile_row(ref, r0, j):
    return ref.at[pl.ds(r0, SUBLANES)].at[pl.ds(j, 1)]


def _wait_rows(tm, copy):
    def wait(r, c):
        copy.wait()
        return c

    lax.fori_loop(0, tm, wait, 0, unroll=8)


def _dispatch_kernel(dest_ref, xg_ref, out_ref, sem, *, tm):
    def copy(r0, j):
        return pltpu.make_async_copy(_tile_row(xg_ref, r0, j), out_ref.at[pl.ds(dest_ref[0, 0, r0 + j], 1)], sem)

    _issue_rows_inline(tm, copy)
    _wait_rows(tm, copy(0, 0))


def _dispatch(xg, dest3):
    t = xg.shape[0]
    tm = dest3.shape[-1]
    return pl.pallas_call(
        functools.partial(_dispatch_kernel, tm=tm),
        grid=(t // tm,),
        in_specs=[pl.BlockSpec((1, 1, tm), lambda i: (i, 0, 0), memory_space=pltpu.SMEM),
                  pl.BlockSpec((tm, XG_WIDTH), lambda i: (i, 0))],
        out_specs=pl.BlockSpec(memory_space=pl.ANY),
        out_shape=jax.ShapeDtypeStruct(xg.shape, _F32),
        scratch_shapes=[pltpu.SemaphoreType.DMA(())],
        compiler_params=_params("arbitrary"),
        name="moe_dispatch",
    )(dest3, xg)


def _expert_kernel(blk_ref, lo_ref, hi_ref, elo_ref, ehi_ref,
                   x_ref, wg_lo, wu_lo, wd_lo, wg_hi, wu_hi, wd_hi, y_ref):
    del elo_ref, ehi_ref
    j = pl.program_id(0)

    @pl.when(jnp.logical_or(j == 0, blk_ref[jnp.maximum(j - 1, 0)] != blk_ref[j]))
    def _():
        y_ref[...] = jnp.zeros_like(y_ref)

    lo = lo_ref[j]
    hi = hi_ref[j]

    @pl.when(hi > lo)
    def _():
        row = lax.broadcasted_iota(jnp.int32, (ROW_BLOCK, 1), 0)
        inside = (row >= lo) & (row < hi)
        xb = x_ref[:, 0:D_MODEL].astype(_BF16)

        def expert(wg, wu, wd):
            gate = jnp.dot(xb, wg[0], preferred_element_type=_F32)
            up = jnp.dot(xb, wu[0], preferred_element_type=_F32)
            hid = gate * jax.nn.sigmoid(gate) * up
            return jnp.dot(hid.astype(_BF16), wd[0], preferred_element_type=_F32)

        g_lo = jnp.where(inside, x_ref[:, D_MODEL + GATE_LO_LANE:D_MODEL + GATE_LO_LANE + 1], 0.0)
        g_hi = jnp.where(inside, x_ref[:, D_MODEL + GATE_HI_LANE:D_MODEL + GATE_HI_LANE + 1], 0.0)
        y_ref[...] += expert(wg_lo, wu_lo, wd_lo) * g_lo + expert(wg_hi, wu_hi, wd_hi) * g_hi


def _experts(x_sorted, item_blk, item_lo, item_hi, item_elo, item_ehi, w_gate, w_up, w_down):
    n_items = item_blk.shape[0]

    def w_spec(shape, which):
        def index(j, blk, lo, hi, elo, ehi):
            return ((elo, ehi)[which][j], 0, 0)
        return pl.BlockSpec(shape, index)

    w_in = (1, D_MODEL, EXPERT_FF)
    w_out = (1, EXPERT_FF, D_MODEL)
    grid_spec = pltpu.PrefetchScalarGridSpec(
        num_scalar_prefetch=5,
        grid=(n_items,),
        in_specs=[pl.BlockSpec((ROW_BLOCK, XG_WIDTH), lambda j, blk, *_: (blk[j], 0)),
                  w_spec(w_in, 0), w_spec(w_in, 0), w_spec(w_out, 0),
                  w_spec(w_in, 1), w_spec(w_in, 1), w_spec(w_out, 1)],
        out_specs=pl.BlockSpec((ROW_BLOCK, D_MODEL), lambda j, blk, *_: (blk[j], 0)),
    )
    return pl.pallas_call(
        _expert_kernel,
        grid_spec=grid_spec,
        out_shape=jax.ShapeDtypeStruct((x_sorted.shape[0], D_MODEL), _F32),
        compiler_params=_params("arbitrary"),
        name="moe_experts",
    )(item_blk, item_lo, item_hi, item_elo, item_ehi, x_sorted, w_gate, w_up, w_down, w_gate, w_up, w_down)


def _expert_rows(dest_ref, dest_next_ref, y_ref, rows_ref, sems, tm, fetch_first=False):
    i = pl.program_id(0)
    slot = i % 2

    def row_copy(d_ref, s):
        def copy(r0, j):
            src = y_ref.at[pl.ds(d_ref[0, 0, r0 + j], 1)]
            return pltpu.make_async_copy(src, _tile_row(rows_ref.at[s], r0, j), sems.at[s])
        return copy

    @pl.when(i == 0)
    def _():
        _issue_rows(tm, row_copy(dest_ref, 0))

    if fetch_first:
        _issue_rows_inline(tm, row_copy(dest_next_ref, 1 - slot))
    _wait_rows(tm, row_copy(dest_ref, slot)(0, 0))
    if fetch_first:
        return rows_ref[slot], lambda: None
    return rows_ref[slot], lambda: _issue_rows_inline(tm, row_copy(dest_next_ref, 1 - slot))


def _expert_rows_finish(dest_ref, y_ref, rows_ref, sems, tm):
    i = pl.program_id(0)

    @pl.when(i == pl.num_programs(0) - 1)
    def _():
        other = 1 - i % 2
        src = y_ref.at[pl.ds(dest_ref[0, 0, 0], 1)]
        _wait_rows(tm, pltpu.make_async_copy(src, _tile_row(rows_ref.at[other], 0, 0), sems.at[other]))


def _expert_rows_specs(n_steps, tm, tile_of_step):
    def dest(offset):
        return pl.BlockSpec((1, 1, tm), lambda i: (tile_of_step(jnp.minimum(i + offset, n_steps - 1)), 0, 0),
                            memory_space=pltpu.SMEM)
    return [dest(0), dest(1), pl.BlockSpec(memory_space=pl.ANY)]


def _expert_rows_scratch(tm):
    return [pltpu.VMEM((2, tm, D_MODEL), _F32), pltpu.SemaphoreType.DMA((2,))]


def _qkv_combine_kernel(dest_ref, dest_next_ref, y_ref, h_ref, g_ref, w_ref,
                        h_out_ref, q_ref, k_ref, v_ref, rows_ref, sems, *, tm):
    rows, fetch_next = _expert_rows(dest_ref, dest_next_ref, y_ref, rows_ref, sems, tm)
    h = h_ref[...] + rows
    fetch_next()
    h_out_ref[...] = h
    _qkv_body(h, g_ref, w_ref, q_ref, k_ref, v_ref)
    _expert_rows_finish(dest_ref, y_ref, rows_ref, sems, tm)


def _qkv_combine(y_sorted, dest, h2, gain, w_qkv):
    t = h2.shape[0]
    tm = _pick_tile(t, (512, 256))
    n = t // tm
    dest3 = dest.reshape(n, 1, tm)
    row = pl.BlockSpec((tm, D_MODEL), lambda i: (i, 0))
    out = jax.ShapeDtypeStruct((t, D_MODEL), _BF16)
    return pl.pallas_call(
        functools.partial(_qkv_combine_kernel, tm=tm),
        grid=(n,),
        in_specs=_expert_rows_specs(n, tm, lambda s: s)
        + [row, _const_spec((1, D_MODEL)), _const_spec((D_MODEL, 3 * D_MODEL))],
        out_specs=[row, row, row, row],
        out_shape=[jax.ShapeDtypeStruct((t, D_MODEL), _F32), out, out, out],
        scratch_shapes=_expert_rows_scratch(tm),
        input_output_aliases={3: 0},
        compiler_params=_params("arbitrary"),
        name="combine_qkv_proj",
    )(dest3, dest3, y_sorted, h2, gain, w_qkv)


def _moe(xg, counts, w_gate, w_up, w_down):
    t = xg.shape[0]
    tm = _pick_tile(t, (1024, 512, 256))
    n_blocks = t // ROW_BLOCK
    cnt = counts[0, :N_CLASSES].astype(jnp.int32)
    ends = jnp.cumsum(cnt)
    starts = ends - cnt
    cuts = jnp.sort(jnp.concatenate([jnp.arange(n_blocks, dtype=jnp.int32) * ROW_BLOCK, starts[1:]]))
    item_blk = jnp.minimum(cuts // ROW_BLOCK, n_blocks - 1)
    item_lo = cuts - item_blk * ROW_BLOCK
    item_hi = jnp.concatenate([cuts[1:], jnp.full((1,), t, jnp.int32)]) - item_blk * ROW_BLOCK
    class_ids = jnp.arange(N_CLASSES, dtype=jnp.int32)

    def lookup(table, idx):
        return jnp.sum(jnp.where(idx[:, None] == class_ids[None, :], table[None, :], 0), axis=1)

    item_cls = jnp.minimum(jnp.sum((ends[None, :] <= cuts[:, None]).astype(jnp.int32), axis=1), N_CLASSES - 1)
    item_elo = lookup(jnp.asarray(_CLASS_LO), item_cls)
    item_ehi = lookup(jnp.asarray(_CLASS_HI), item_cls)
    cls = xg[:, D_MODEL + CLASS_LANE].astype(jnp.int32)
    rank = xg[:, D_MODEL + RANK_LANE].astype(jnp.int32)
    start_of = lookup(starts, cls)
    dest = (start_of + rank).astype(jnp.int32)
    x_sorted = _dispatch(xg, dest.reshape(t // tm, 1, tm))
    y_sorted = _experts(x_sorted, item_blk, item_lo, item_hi, item_elo, item_ehi, w_gate, w_up, w_down)
    return y_sorted, dest


def _final_kernel(dest_ref, dest_next_ref, y_ref, h_ref, g_ref, o_ref, rows_ref, sems, *, tm):
    rows, _ = _expert_rows(dest_ref, dest_next_ref, y_ref, rows_ref, sems, tm, fetch_first=True)
    o_ref[...] = _rms(h_ref[...] + rows, g_ref[...])
    _expert_rows_finish(dest_ref, y_ref, rows_ref, sems, tm)


def _final(y_sorted, dest, h2, gain, b, seq):
    tm = ATTN_TILE
    per_batch = seq // tm
    n = b * per_batch

    def tile_of_step(s):
        return (s // per_batch) * (per_batch + 1) + s % per_batch + 1

    out = pl.pallas_call(
        functools.partial(_final_kernel, tm=tm),
        grid=(n,),
        in_specs=_expert_rows_specs(n, tm, tile_of_step)
        + [pl.BlockSpec((tm, D_MODEL), lambda i: (tile_of_step(i), 0)), _const_spec((1, D_MODEL))],
        out_specs=pl.BlockSpec((tm, D_MODEL), lambda i: (i, 0)),
        out_shape=jax.ShapeDtypeStruct((b * seq, D_MODEL), _F32),
        scratch_shapes=_expert_rows_scratch(tm),
        compiler_params=_params("arbitrary"),
        name="final_norm",
    )(dest.reshape(-1, 1, tm), dest.reshape(-1, 1, tm), y_sorted, h2, gain)
    return out.reshape(b, seq, D_MODEL)


def kernel(x, meta_tokens, mix_norm, ffn_norm, final_norm, sb_w_qkv, sb_w_o, pool_w, pool_scale,
           router_group_w, router_group_b, router_expert_w, router_expert_b,
           expert_w_gate, expert_w_up, expert_w_down):
    b, seq, d = x.shape
    assert d == D_MODEL and seq % ATTN_TILE == 0
    depth = mix_norm.shape[0]
    p = seq + ATTN_TILE
    t = b * p
    meta = jnp.broadcast_to(meta_tokens[None].astype(x.dtype), (b, N_META, d))
    h2 = jnp.concatenate([jnp.zeros((b, FRONT, d), x.dtype), meta, x], axis=1).reshape(t, d)
    moe_out = None

    for i in range(depth):
        j = i // 2
        wr, br = _router_weights(router_group_w[i], router_group_b[i], router_expert_w[i], router_expert_b[i])
        ffn_gain = ffn_norm[i].reshape(1, d)
        mix_gain = mix_norm[i].reshape(1, d)
        if i % 2 == 0:
            w_qkv = sb_w_qkv[j].astype(_BF16)
            if moe_out is None:
                q, k, v = _qkv(h2, mix_gain, w_qkv)
            else:
                h2, q, k, v = _qkv_combine(*moe_out, h2, mix_gain, w_qkv)
            o = _attention(q.reshape(b, p, d), k.reshape(b, p, d), v.reshape(b, p, d))
            h2, xg, counts = _post_attn(o.reshape(t, d), h2, sb_w_o[j].astype(_BF16), ffn_gain, wr, br)
        else:
            assert moe_out is not None
            h2, xg, counts = _pool(*moe_out, h2, p, mix_gain, pool_w[j].astype(_BF16),
                                   pool_scale[j].reshape(1, d), ffn_gain, wr, br)
        moe_out = _moe(xg, counts, expert_w_gate[i].astype(_BF16), expert_w_up[i].astype(_BF16),
                       expert_w_down[i].astype(_BF16))
    return _final(*moe_out, h2, final_norm.reshape(1, d), b, seq)
```

```python
import functools

import numpy as np
import jax
import jax.numpy as jnp
from jax import lax
from jax.experimental import pallas as pl
from jax.experimental.pallas import tpu as pltpu

D_MODEL = 1024
N_META = 16
HEADS = 8
HEAD_DIM = D_MODEL // HEADS
POOL_WINDOWS = (2, 4, 8, 16)
POOL_GROUP_DIM = D_MODEL // len(POOL_WINDOWS)
POOL_HALO = 16
MOE_GROUPS = 4
EXPERTS_PER_GROUP = 8
N_EXPERTS = MOE_GROUPS * EXPERTS_PER_GROUP
EXPERT_FF = D_MODEL // 4
PAIRS_PER_GROUP = EXPERTS_PER_GROUP * (EXPERTS_PER_GROUP - 1) // 2
N_CLASSES = MOE_GROUPS * PAIRS_PER_GROUP
RMS_EPS = 1e-6

LANES = 128
SUBLANES = 8
DMA_THREADS = 2
ATTN_TILE = 256
FRONT = ATTN_TILE - N_META
HEADS_PER_STEP = 8
LOG2_E = 1.4426950408889634
EXHAUSTED_LOG2 = -200.0
ROW_BLOCK = 256
HALF = D_MODEL // 2
XG_WIDTH = HALF + LANES
_U32 = jnp.uint32
HIGH_HALF_MASK = 0xFFFF0000
GATE_LO_LANE, GATE_HI_LANE, CLASS_LANE, RANK_LANE = 0, 1, 2, 3
NEG_BIG = -1e30
VMEM_LIMIT = 48 * 1024 * 1024

_F32 = jnp.float32
_BF16 = jnp.bfloat16


def _class_tables():
    lo, hi = [], []
    for g in range(MOE_GROUPS):
        for a in range(EXPERTS_PER_GROUP):
            for b in range(a + 1, EXPERTS_PER_GROUP):
                lo.append(g * EXPERTS_PER_GROUP + a)
                hi.append(g * EXPERTS_PER_GROUP + b)
    return np.asarray(lo, np.int32), np.asarray(hi, np.int32)


_CLASS_LO, _CLASS_HI = _class_tables()


def _pick_tile(n, candidates):
    for c in candidates:
        if n % c == 0:
            return c
    raise ValueError(f"no tile in {candidates} divides {n}")


def _params(*semantics):
    return pltpu.CompilerParams(dimension_semantics=semantics, vmem_limit_bytes=VMEM_LIMIT)


def _rms(x, g):
    return x * lax.rsqrt(jnp.mean(x * x, axis=-1, keepdims=True) + RMS_EPS) * g


def _qkv_kernel(h_ref, g_ref, w_ref, q_ref, k_ref, v_ref):
    _qkv_body(h_ref[...], g_ref, w_ref, q_ref, k_ref, v_ref)


def _qkv_body(h, g_ref, w_ref, q_ref, k_ref, v_ref):
    xn = _rms(h, g_ref[...]).astype(_BF16)
    scale = HEAD_DIM ** -0.5 * LOG2_E
    q = jnp.dot(xn, w_ref[:, 0:D_MODEL], preferred_element_type=_F32)
    q_ref[...] = (q * scale).astype(q_ref.dtype)
    k = jnp.dot(xn, w_ref[:, D_MODEL:2 * D_MODEL], preferred_element_type=_F32)
    k_ref[...] = k.astype(k_ref.dtype)
    v = jnp.dot(xn, w_ref[:, 2 * D_MODEL:3 * D_MODEL], preferred_element_type=_F32)
    v_ref[...] = v.astype(v_ref.dtype)


def _qkv(h2, gain, w_qkv):
    t = h2.shape[0]
    tm = _pick_tile(t, (512, 256))
    row = pl.BlockSpec((tm, D_MODEL), lambda i: (i, 0))
    out = jax.ShapeDtypeStruct((t, D_MODEL), _BF16)
    return pl.pallas_call(
        _qkv_kernel,
        grid=(t // tm,),
        in_specs=[row, pl.BlockSpec((1, D_MODEL), lambda i: (0, 0)),
                  pl.BlockSpec((D_MODEL, 3 * D_MODEL), lambda i: (0, 0))],
        out_specs=[row, row, row],
        out_shape=[out, out, out],
        compiler_params=_params("parallel"),
        name="qkv_proj",
    )(h2, gain, w_qkv)


def _neg_abs(x):
    bits = lax.bitcast_convert_type(x, jnp.uint32) | jnp.uint32(0x80000000)
    return lax.bitcast_convert_type(bits, _F32)


def _attn_kernel(q_ref, k_ref, v_ref, o_ref, acc_ref, carry_ref):
    qi = pl.program_id(2)
    tri_r = lax.broadcasted_iota(jnp.int32, (ATTN_TILE, ATTN_TILE), 0)
    tri_c = lax.broadcasted_iota(jnp.int32, (ATTN_TILE, ATTN_TILE), 1)
    neg_suffix = jnp.where(tri_r > tri_c, -1.0, 0.0).astype(_BF16)
    acc_ref[...] = jnp.zeros_like(acc_ref)
    carry_ref[...] = jnp.zeros_like(carry_ref)

    def step(kb, masked):
        start = pl.multiple_of(kb * ATTN_TILE, ATTN_TILE)
        if masked:
            q_pos = qi * ATTN_TILE + tri_r
            k_pos = kb * ATTN_TILE + tri_c
            mask = (k_pos < q_pos) & (k_pos >= FRONT)
        for g in range(HEADS_PER_STEP):
            cols = slice(g * HEAD_DIM, (g + 1) * HEAD_DIM)
            k = k_ref[0, pl.ds(start, ATTN_TILE), cols]
            v = v_ref[0, pl.ds(start, ATTN_TILE), cols]
            z = lax.dot_general(q_ref[0, :, cols], k, (((1,), (1,)), ((), ())), preferred_element_type=_F32)
            sp = jnp.maximum(z, 0.0) + jnp.log(1.0 + jnp.exp2(_neg_abs(z))) * LOG2_E
            if masked:
                sp = jnp.where(mask, sp, 0.0)
            within = jnp.dot(sp.astype(_BF16), neg_suffix, preferred_element_type=_F32)
            later = within + carry_ref[g]
            a = jnp.exp2(z - sp + later)
            if masked:
                a = jnp.where(mask, a, 0.0)
            acc_ref[g] += jnp.dot(a.astype(_BF16), v, preferred_element_type=_F32)
            carry_ref[g] = later[:, 0:1] - sp[:, 0:1]

    def alive():
        top = carry_ref[0]
        for g in range(1, HEADS_PER_STEP):
            top = jnp.maximum(top, carry_ref[g])
        return (jnp.max(top) > EXHAUSTED_LOG2).astype(jnp.int32)

    step(qi, True)

    def cond(state):
        kb, live = state
        return jnp.logical_and(kb >= 1, live > 0)

    def body(state):
        kb, _ = state
        step(kb, False)
        return kb - 1, alive()

    kb_end, live = lax.while_loop(cond, body, (qi - 1, alive()))

    @pl.when(jnp.logical_and(kb_end == 0, live > 0))
    def _():
        step(0, True)

    for g in range(HEADS_PER_STEP):
        o_ref[0, :, g * HEAD_DIM:(g + 1) * HEAD_DIM] = acc_ref[g].astype(o_ref.dtype)


def _attention(q, k, v):
    b, p, _ = q.shape
    nq = p // ATTN_TILE
    width = HEADS_PER_STEP * HEAD_DIM
    qspec = pl.BlockSpec((1, ATTN_TILE, width), lambda bi, hi, qi: (bi, qi, hi))
    kvspec = pl.BlockSpec((1, p, width), lambda bi, hi, qi: (bi, 0, hi), pipeline_mode=pl.Buffered(1))
    return pl.pallas_call(
        _attn_kernel,
        grid=(b, HEADS // HEADS_PER_STEP, nq),
        in_specs=[qspec, kvspec, kvspec],
        out_specs=qspec,
        out_shape=jax.ShapeDtypeStruct(q.shape, _BF16),
        scratch_shapes=[pltpu.VMEM((HEADS_PER_STEP, ATTN_TILE, HEAD_DIM), _F32),
                        pltpu.VMEM((HEADS_PER_STEP, ATTN_TILE, 1), _F32)],
        compiler_params=_params("parallel", "parallel", "arbitrary"),
        name="sb_attention",
    )(q, k, v)


def _route(h_new, g_ref, wr_ref, br_ref, cnt_ref, xg_ref, counts_ref):
    tm = h_new.shape[0]
    xn = _rms(h_new, g_ref[...])
    x_hi = xn.astype(_BF16)
    x_lo = (xn - x_hi.astype(_F32)).astype(_BF16)
    both = jnp.dot(x_hi, wr_ref[...], preferred_element_type=_F32)
    logits = (both[:, 0:LANES] + both[:, LANES:2 * LANES]
              + jnp.dot(x_lo, wr_ref[:, 0:LANES], preferred_element_type=_F32)) + br_ref[...]
    lane = lax.broadcasted_iota(jnp.int32, (tm, LANES), 1).astype(_F32)

    def first_argmax(vals, vmax):
        return jnp.min(jnp.where(vals >= vmax, lane, float(LANES)), axis=-1, keepdims=True)

    gl = jnp.where(lane < MOE_GROUPS, logits, NEG_BIG)
    gmax = jnp.max(gl, axis=-1, keepdims=True)
    gsel = first_argmax(gl, gmax)
    gprob = 1.0 / jnp.sum(jnp.exp(gl - gmax), axis=-1, keepdims=True)
    first = MOE_GROUPS + EXPERTS_PER_GROUP * gsel
    el = jnp.where((lane >= first) & (lane < first + EXPERTS_PER_GROUP), logits, NEG_BIG)
    m1 = jnp.max(el, axis=-1, keepdims=True)
    i1 = first_argmax(el, m1)
    el2 = jnp.where(lane == i1, NEG_BIG, el)
    m2 = jnp.max(el2, axis=-1, keepdims=True)
    i2 = first_argmax(el2, m2)
    ratio = jnp.exp(m2 - m1)
    gate1 = gprob / (1.0 + ratio)
    gate2 = gprob * ratio / (1.0 + ratio)
    a = jnp.minimum(i1, i2) - first
    bb = jnp.maximum(i1, i2) - first
    pair = (EXPERTS_PER_GROUP - 1) * a - a * (a - 1.0) * 0.5 + (bb - a - 1.0)
    cls = gsel * PAIRS_PER_GROUP + pair
    gate_lo = jnp.where(i1 < i2, gate1, gate2)
    gate_hi = jnp.where(i1 < i2, gate2, gate1)

    onehot = (lane == cls).astype(_F32)
    tri_r = lax.broadcasted_iota(jnp.int32, (tm, tm), 0)
    tri_c = lax.broadcasted_iota(jnp.int32, (tm, tm), 1)
    before = (tri_c < tri_r).astype(_BF16)
    seen = jnp.dot(before, onehot.astype(_BF16), preferred_element_type=_F32) + cnt_ref[...]
    rank = jnp.sum(onehot * seen, axis=-1, keepdims=True)
    cnt_ref[...] += jnp.sum(onehot, axis=0, keepdims=True)
    counts_ref[...] = cnt_ref[...]

    extra = jnp.where(lane == GATE_LO_LANE, gate_lo,
                      jnp.where(lane == GATE_HI_LANE, gate_hi,
                                jnp.where(lane == CLASS_LANE, cls,
                                          jnp.where(lane == RANK_LANE, rank, 0.0))))
    low = lax.bitcast_convert_type(x_hi[:, 0:HALF].astype(_F32), _U32)
    high = lax.bitcast_convert_type(x_hi[:, HALF:D_MODEL].astype(_F32), _U32)
    xg_ref[:, 0:HALF] = high | (low >> 16)
    xg_ref[:, HALF:XG_WIDTH] = lax.bitcast_convert_type(extra, _U32)


def _router_weights(wg, bg, we, be):
    w = jnp.zeros((D_MODEL, LANES), _F32)
    w = w.at[:, 0:MOE_GROUPS].set(wg).at[:, MOE_GROUPS:MOE_GROUPS + N_EXPERTS].set(we)
    bias = jnp.zeros((1, LANES), _F32)
    bias = bias.at[0, 0:MOE_GROUPS].set(bg).at[0, MOE_GROUPS:MOE_GROUPS + N_EXPERTS].set(be)
    w_hi = w.astype(_BF16)
    w_lo = (w - w_hi.astype(_F32)).astype(_BF16)
    return jnp.concatenate([w_hi, w_lo], axis=1), bias


def _post_attn_kernel(o_ref, h_ref, wo_ref, g_ref, wr_ref, br_ref,
                      h_out_ref, xg_ref, counts_ref, cnt_ref):
    @pl.when(pl.program_id(0) == 0)
    def _():
        cnt_ref[...] = jnp.zeros_like(cnt_ref)

    h_new = h_ref[...] + jnp.dot(o_ref[...], wo_ref[...], preferred_element_type=_F32)
    h_out_ref[...] = h_new
    _route(h_new, g_ref, wr_ref, br_ref, cnt_ref, xg_ref, counts_ref)


def _const_spec(shape):
    return pl.BlockSpec(shape, lambda *_: (0,) * len(shape))


def _post_attn(o2, h2, w_o, gain, wr, br):
    t = h2.shape[0]
    tm = _pick_tile(t, (512, 256))
    row = pl.BlockSpec((tm, D_MODEL), lambda i: (i, 0))
    return pl.pallas_call(
        _post_attn_kernel,
        grid=(t // tm,),
        in_specs=[row, row, _const_spec((D_MODEL, D_MODEL)), _const_spec((1, D_MODEL)),
                  _const_spec((D_MODEL, 2 * LANES)), _const_spec((1, LANES))],
        out_specs=[row, pl.BlockSpec((tm, XG_WIDTH), lambda i: (i, 0)), _const_spec((1, LANES))],
        out_shape=[jax.ShapeDtypeStruct((t, D_MODEL), _F32), jax.ShapeDtypeStruct((t, XG_WIDTH), _U32),
                   jax.ShapeDtypeStruct((1, LANES), _F32)],
        scratch_shapes=[pltpu.VMEM((1, LANES), _F32)],
        compiler_params=_params("arbitrary"),
        name="attn_out_route",
    )(o2, h2, w_o, gain, wr, br)


def _pool_kernel(dest_ref, dest_next_ref, y_ref, h_ref, mg_ref, pw_ref, ps_ref, g_ref, wr_ref, br_ref,
                 h_out_ref, xg_ref, counts_ref, rows_ref, sems, cnt_ref, tail_ref, *, tile, tiles_per_seq):
    step = pl.program_id(0)
    i = step % tiles_per_seq

    @pl.when(step == 0)
    def _():
        cnt_ref[...] = jnp.zeros_like(cnt_ref)
        tail_ref[...] = jnp.zeros_like(tail_ref)

    rows, fetch_next = _expert_rows(dest_ref, dest_next_ref, y_ref, rows_ref, sems, tile)
    h = h_ref[...] + rows
    fetch_next()
    hn = _rms(h, mg_ref[...])
    halo = jnp.where(i == 0, 0.0, tail_ref[...])
    tail_ref[...] = hn[tile - POOL_HALO:, :]
    cat = jnp.concatenate([halo, hn], axis=0)
    token = i * tile + lax.broadcasted_iota(jnp.int32, (tile, 1), 0) - FRONT
    outs = []
    for gi, w in enumerate(POOL_WINDOWS):
        cols = slice(gi * POOL_GROUP_DIM, (gi + 1) * POOL_GROUP_DIM)
        xg = hn[:, cols]
        run = cat[:, cols]
        shift = 1
        while shift < w:
            run = run + pltpu.roll(run, shift, axis=0)
            shift *= 2
        wsum = run[POOL_HALO:, :]
        count = jnp.clip(token + 1, 1, w).astype(_F32)
        mixed = wsum / count - xg
        outs.append(jnp.dot(mixed.astype(_BF16), pw_ref[gi], preferred_element_type=_F32))
    h_new = h + jnp.concatenate(outs, axis=1) * ps_ref[...]
    h_out_ref[...] = h_new
    _route(h_new, g_ref, wr_ref, br_ref, cnt_ref, xg_ref, counts_ref)
    _expert_rows_finish(dest_ref, y_ref, rows_ref, sems, tile)


def _pool(y_sorted, dest, h2, seq_rows, mix_gain, pool_w, pool_scale, gain, wr, br):
    t = h2.shape[0]
    tile = _pick_tile(seq_rows, (768, 256))
    n = t // tile
    dest3 = dest.reshape(n, 1, tile)
    row = pl.BlockSpec((tile, D_MODEL), lambda s: (s, 0))
    return pl.pallas_call(
        functools.partial(_pool_kernel, tile=tile, tiles_per_seq=seq_rows // tile),
        grid=(n,),
        in_specs=_expert_rows_specs(n, tile, lambda s: s)
        + [row, _const_spec((1, D_MODEL)),
           _const_spec((len(POOL_WINDOWS), POOL_GROUP_DIM, POOL_GROUP_DIM)), _const_spec((1, D_MODEL)),
           _const_spec((1, D_MODEL)), _const_spec((D_MODEL, 2 * LANES)), _const_spec((1, LANES))],
        out_specs=[row, pl.BlockSpec((tile, XG_WIDTH), lambda s: (s, 0)), _const_spec((1, LANES))],
        out_shape=[jax.ShapeDtypeStruct(h2.shape, _F32), jax.ShapeDtypeStruct((t, XG_WIDTH), _U32),
                   jax.ShapeDtypeStruct((1, LANES), _F32)],
        scratch_shapes=_expert_rows_scratch(tile)
        + [pltpu.VMEM((1, LANES), _F32), pltpu.VMEM((POOL_HALO, D_MODEL), _F32)],
        input_output_aliases={3: 0},
        compiler_params=_params("arbitrary"),
        name="combine_pool_route",
    )(dest3, dest3, y_sorted, h2, mix_gain, pool_w, pool_scale, gain, wr, br)


def _issue_rows(tm, copy_for_row):
    def group(g, c):
        r0 = pl.multiple_of(g * SUBLANES, SUBLANES)
        for j in range(SUBLANES):
            copy_for_row(r0, j).start(priority=j % DMA_THREADS)
        return c

    lax.fori_loop(0, tm // SUBLANES, group, 0)


def _issue_rows_inline(tm, copy_for_row):
    for r0 in range(0, tm, SUBLANES):
        for j in range(SUBLANES):
            copy_for_row(r0, j).start(priority=j % DMA_THREADS)


def _tile_row(ref, r0, j):
    return ref.at[pl.ds(r0, SUBLANES)].at[pl.ds(j, 1)]


def _wait_rows(tm, copy):
    def wait(r, c):
        copy.wait()
        return c

    lax.fori_loop(0, tm, wait, 0, unroll=8)


def _dispatch_kernel(dest_ref, xg_ref, out_ref, sem, *, tm):
    def copy(r0, j):
        return pltpu.make_async_copy(_tile_row(xg_ref, r0, j), out_ref.at[pl.ds(dest_ref[0, 0, r0 + j], 1)], sem)

    _issue_rows_inline(tm, copy)
    _wait_rows(tm, copy(0, 0))


def _dispatch(xg, dest3):
    t = xg.shape[0]
    tm = dest3.shape[-1]
    return pl.pallas_call(
        functools.partial(_dispatch_kernel, tm=tm),
        grid=(t // tm,),
        in_specs=[pl.BlockSpec((1, 1, tm), lambda i: (i, 0, 0), memory_space=pltpu.SMEM),
                  pl.BlockSpec((tm, XG_WIDTH), lambda i: (i, 0))],
        out_specs=pl.BlockSpec(memory_space=pl.ANY),
        out_shape=jax.ShapeDtypeStruct(xg.shape, xg.dtype),
        scratch_shapes=[pltpu.SemaphoreType.DMA(())],
        compiler_params=_params("arbitrary"),
        name="moe_dispatch",
    )(dest3, xg)


def _expert_kernel(blk_ref, lo_ref, hi_ref, elo_ref, ehi_ref,
                   x_ref, wg_lo, wu_lo, wd_lo, wg_hi, wu_hi, wd_hi, y_ref):
    del elo_ref, ehi_ref
    j = pl.program_id(0)

    @pl.when(jnp.logical_or(j == 0, blk_ref[jnp.maximum(j - 1, 0)] != blk_ref[j]))
    def _():
        y_ref[...] = jnp.zeros_like(y_ref)

    lo = lo_ref[j]
    hi = hi_ref[j]

    @pl.when(hi > lo)
    def _():
        row = lax.broadcasted_iota(jnp.int32, (ROW_BLOCK, 1), 0)
        inside = (row >= lo) & (row < hi)
        words = x_ref[:, 0:HALF]
        low = lax.bitcast_convert_type(words << 16, _F32)
        high = lax.bitcast_convert_type(words & _U32(HIGH_HALF_MASK), _F32)
        xb = jnp.concatenate([low, high], axis=1).astype(_BF16)
        gates = lax.bitcast_convert_type(x_ref[:, HALF:XG_WIDTH], _F32)

        def expert(wg, wu, wd):
            gate = jnp.dot(xb, wg[0], preferred_element_type=_F32)
            up = jnp.dot(xb, wu[0], preferred_element_type=_F32)
            hid = gate * jax.nn.sigmoid(gate) * up
            return jnp.dot(hid.astype(_BF16), wd[0], preferred_element_type=_F32)

        g_lo = jnp.where(inside, gates[:, GATE_LO_LANE:GATE_LO_LANE + 1], 0.0)
        g_hi = jnp.where(inside, gates[:, GATE_HI_LANE:GATE_HI_LANE + 1], 0.0)
        y_ref[...] += expert(wg_lo, wu_lo, wd_lo) * g_lo + expert(wg_hi, wu_hi, wd_hi) * g_hi


def _experts(x_sorted, item_blk, item_lo, item_hi, item_elo, item_ehi, w_gate, w_up, w_down):
    n_items = item_blk.shape[0]

    def w_spec(shape, which):
        def index(j, blk, lo, hi, elo, ehi):
            return ((elo, ehi)[which][j], 0, 0)
        return pl.BlockSpec(shape, index)

    w_in = (1, D_MODEL, EXPERT_FF)
    w_out = (1, EXPERT_FF, D_MODEL)
    grid_spec = pltpu.PrefetchScalarGridSpec(
        num_scalar_prefetch=5,
        grid=(n_items,),
        in_specs=[pl.BlockSpec((ROW_BLOCK, XG_WIDTH), lambda j, blk, *_: (blk[j], 0)),
                  w_spec(w_in, 0), w_spec(w_in, 0), w_spec(w_out, 0),
                  w_spec(w_in, 1), w_spec(w_in, 1), w_spec(w_out, 1)],
        out_specs=pl.BlockSpec((ROW_BLOCK, D_MODEL), lambda j, blk, *_: (blk[j], 0)),
    )
    return pl.pallas_call(
        _expert_kernel,
        grid_spec=grid_spec,
        out_shape=jax.ShapeDtypeStruct((x_sorted.shape[0], D_MODEL), _F32),
        compiler_params=_params("arbitrary"),
        name="moe_experts",
    )(item_blk, item_lo, item_hi, item_elo, item_ehi, x_sorted, w_gate, w_up, w_down, w_gate, w_up, w_down)


def _expert_rows(dest_ref, dest_next_ref, y_ref, rows_ref, sems, tm, fetch_first=False):
    i = pl.program_id(0)
    slot = i % 2

    def row_copy(d_ref, s):
        def copy(r0, j):
            src = y_ref.at[pl.ds(d_ref[0, 0, r0 + j], 1)]
            return pltpu.make_async_copy(src, _tile_row(rows_ref.at[s], r0, j), sems.at[s])
        return copy

    @pl.when(i == 0)
    def _():
        _issue_rows(tm, row_copy(dest_ref, 0))

    if fetch_first:
        _issue_rows_inline(tm, row_copy(dest_next_ref, 1 - slot))
    _wait_rows(tm, row_copy(dest_ref, slot)(0, 0))
    if fetch_first:
        return rows_ref[slot], lambda: None
    return rows_ref[slot], lambda: _issue_rows_inline(tm, row_copy(dest_next_ref, 1 - slot))


def _expert_rows_finish(dest_ref, y_ref, rows_ref, sems, tm):
    i = pl.program_id(0)

    @pl.when(i == pl.num_programs(0) - 1)
    def _():
        other = 1 - i % 2
        src = y_ref.at[pl.ds(dest_ref[0, 0, 0], 1)]
        _wait_rows(tm, pltpu.make_async_copy(src, _tile_row(rows_ref.at[other], 0, 0), sems.at[other]))


def _expert_rows_specs(n_steps, tm, tile_of_step):
    def dest(offset):
        return pl.BlockSpec((1, 1, tm), lambda i: (tile_of_step(jnp.minimum(i + offset, n_steps - 1)), 0, 0),
                            memory_space=pltpu.SMEM)
    return [dest(0), dest(1), pl.BlockSpec(memory_space=pl.ANY)]


def _expert_rows_scratch(tm):
    return [pltpu.VMEM((2, tm, D_MODEL), _F32), pltpu.SemaphoreType.DMA((2,))]


def _qkv_combine_kernel(dest_ref, dest_next_ref, y_ref, h_ref, g_ref, w_ref,
                        h_out_ref, q_ref, k_ref, v_ref, rows_ref, sems, *, tm):
    rows, fetch_next = _expert_rows(dest_ref, dest_next_ref, y_ref, rows_ref, sems, tm)
    h = h_ref[...] + rows
    fetch_next()
    h_out_ref[...] = h
    _qkv_body(h, g_ref, w_ref, q_ref, k_ref, v_ref)
    _expert_rows_finish(dest_ref, y_ref, rows_ref, sems, tm)


def _qkv_combine(y_sorted, dest, h2, gain, w_qkv):
    t = h2.shape[0]
    tm = _pick_tile(t, (512, 256))
    n = t // tm
    dest3 = dest.reshape(n, 1, tm)
    row = pl.BlockSpec((tm, D_MODEL), lambda i: (i, 0))
    out = jax.ShapeDtypeStruct((t, D_MODEL), _BF16)
    return pl.pallas_call(
        functools.partial(_qkv_combine_kernel, tm=tm),
        grid=(n,),
        in_specs=_expert_rows_specs(n, tm, lambda s: s)
        + [row, _const_spec((1, D_MODEL)), _const_spec((D_MODEL, 3 * D_MODEL))],
        out_specs=[row, row, row, row],
        out_shape=[jax.ShapeDtypeStruct((t, D_MODEL), _F32), out, out, out],
        scratch_shapes=_expert_rows_scratch(tm),
        input_output_aliases={3: 0},
        compiler_params=_params("arbitrary"),
        name="combine_qkv_proj",
    )(dest3, dest3, y_sorted, h2, gain, w_qkv)


def _moe(xg, counts, w_gate, w_up, w_down):
    t = xg.shape[0]
    tm = _pick_tile(t, (1024, 512, 256))
    n_blocks = t // ROW_BLOCK
    cnt = counts[0, :N_CLASSES].astype(jnp.int32)
    ends = jnp.cumsum(cnt)
    starts = ends - cnt
    cuts = jnp.sort(jnp.concatenate([jnp.arange(n_blocks, dtype=jnp.int32) * ROW_BLOCK, starts[1:]]))
    item_blk = jnp.minimum(cuts // ROW_BLOCK, n_blocks - 1)
    item_lo = cuts - item_blk * ROW_BLOCK
    item_hi = jnp.concatenate([cuts[1:], jnp.full((1,), t, jnp.int32)]) - item_blk * ROW_BLOCK
    class_ids = jnp.arange(N_CLASSES, dtype=jnp.int32)

    def lookup(table, idx):
        return jnp.sum(jnp.where(idx[:, None] == class_ids[None, :], table[None, :], 0), axis=1)

    item_cls = jnp.minimum(jnp.sum((ends[None, :] <= cuts[:, None]).astype(jnp.int32), axis=1), N_CLASSES - 1)
    item_elo = lookup(jnp.asarray(_CLASS_LO), item_cls)
    item_ehi = lookup(jnp.asarray(_CLASS_HI), item_cls)
    cls = lax.bitcast_convert_type(xg[:, HALF + CLASS_LANE], _F32).astype(jnp.int32)
    rank = lax.bitcast_convert_type(xg[:, HALF + RANK_LANE], _F32).astype(jnp.int32)
    start_of = lookup(starts, cls)
    dest = (start_of + rank).astype(jnp.int32)
    x_sorted = _dispatch(xg, dest.reshape(t // tm, 1, tm))
    y_sorted = _experts(x_sorted, item_blk, item_lo, item_hi, item_elo, item_ehi, w_gate, w_up, w_down)
    return y_sorted, dest


def _final_kernel(dest_ref, dest_next_ref, y_ref, h_ref, g_ref, o_ref, rows_ref, sems, *, tm):
    rows, _ = _expert_rows(dest_ref, dest_next_ref, y_ref, rows_ref, sems, tm, fetch_first=True)
    o_ref[...] = _rms(h_ref[...] + rows, g_ref[...])
    _expert_rows_finish(dest_ref, y_ref, rows_ref, sems, tm)


def _final(y_sorted, dest, h2, gain, b, seq):
    tm = ATTN_TILE
    per_batch = seq // tm
    n = b * per_batch

    def tile_of_step(s):
        return (s // per_batch) * (per_batch + 1) + s % per_batch + 1

    out = pl.pallas_call(
        functools.partial(_final_kernel, tm=tm),
        grid=(n,),
        in_specs=_expert_rows_specs(n, tm, tile_of_step)
        + [pl.BlockSpec((tm, D_MODEL), lambda i: (tile_of_step(i), 0)), _const_spec((1, D_MODEL))],
        out_specs=pl.BlockSpec((tm, D_MODEL), lambda i: (i, 0)),
        out_shape=jax.ShapeDtypeStruct((b * seq, D_MODEL), _F32),
        scratch_shapes=_expert_rows_scratch(tm),
        compiler_params=_params("arbitrary"),
        name="final_norm",
    )(dest.reshape(-1, 1, tm), dest.reshape(-1, 1, tm), y_sorted, h2, gain)
    return out.reshape(b, seq, D_MODEL)


def kernel(x, meta_tokens, mix_norm, ffn_norm, final_norm, sb_w_qkv, sb_w_o, pool_w, pool_scale,
           router_group_w, router_group_b, router_expert_w, router_expert_b,
           expert_w_gate, expert_w_up, expert_w_down):
    b, seq, d = x.shape
    assert d == D_MODEL and seq % ATTN_TILE == 0
    depth = mix_norm.shape[0]
    p = seq + ATTN_TILE
    t = b * p
    meta = jnp.broadcast_to(meta_tokens[None].astype(x.dtype), (b, N_META, d))
    h2 = jnp.concatenate([jnp.zeros((b, FRONT, d), x.dtype), meta, x], axis=1).reshape(t, d)
    moe_out = None

    for i in range(depth):
        j = i // 2
        wr, br = _router_weights(router_group_w[i], router_group_b[i], router_expert_w[i], router_expert_b[i])
        ffn_gain = ffn_norm[i].reshape(1, d)
        mix_gain = mix_norm[i].reshape(1, d)
        if i % 2 == 0:
            w_qkv = sb_w_qkv[j].astype(_BF16)
            if moe_out is None:
                q, k, v = _qkv(h2, mix_gain, w_qkv)
            else:
                h2, q, k, v = _qkv_combine(*moe_out, h2, mix_gain, w_qkv)
            o = _attention(q.reshape(b, p, d), k.reshape(b, p, d), v.reshape(b, p, d))
            h2, xg, counts = _post_attn(o.reshape(t, d), h2, sb_w_o[j].astype(_BF16), ffn_gain, wr, br)
        else:
            assert moe_out is not None
            h2, xg, counts = _pool(*moe_out, h2, p, mix_gain, pool_w[j].astype(_BF16),
                                   pool_scale[j].reshape(1, d), ffn_gain, wr, br)
        moe_out = _moe(xg, counts, expert_w_gate[i].astype(_BF16), expert_w_up[i].astype(_BF16),
                       expert_w_down[i].astype(_BF16))
    return _final(*moe_out, h2, final_norm.reshape(1, d), b, seq)
```

```python
import functools

import numpy as np
import jax
import jax.numpy as jnp
from jax import lax
from jax.experimental import pallas as pl
from jax.experimental.pallas import tpu as pltpu

D_MODEL = 1024
N_META = 16
HEADS = 8
HEAD_DIM = D_MODEL // HEADS
POOL_WINDOWS = (2, 4, 8, 16)
POOL_GROUP_DIM = D_MODEL // len(POOL_WINDOWS)
POOL_HALO = 16
MOE_GROUPS = 4
EXPERTS_PER_GROUP = 8
N_EXPERTS = MOE_GROUPS * EXPERTS_PER_GROUP
EXPERT_FF = D_MODEL // 4
PAIRS_PER_GROUP = EXPERTS_PER_GROUP * (EXPERTS_PER_GROUP - 1) // 2
N_CLASSES = MOE_GROUPS * PAIRS_PER_GROUP
RMS_EPS = 1e-6

LANES = 128
SUBLANES = 8
DMA_THREADS = 2
ATTN_TILE = 256
FRONT = ATTN_TILE - N_META
HEADS_PER_STEP = 8
LOG2_E = 1.4426950408889634
EXHAUSTED_LOG2 = -200.0
ROW_BLOCK = 256
HALF = D_MODEL // 2
XG_WIDTH = HALF + LANES
_U32 = jnp.uint32
HIGH_HALF_MASK = 0xFFFF0000
GATE_LO_LANE, GATE_HI_LANE, CLASS_LANE, RANK_LANE = 0, 1, 2, 3
NEG_BIG = -1e30
VMEM_LIMIT = 48 * 1024 * 1024

_F32 = jnp.float32
_BF16 = jnp.bfloat16


def _class_tables():
    lo, hi = [], []
    for g in range(MOE_GROUPS):
        for a in range(EXPERTS_PER_GROUP):
            for b in range(a + 1, EXPERTS_PER_GROUP):
                lo.append(g * EXPERTS_PER_GROUP + a)
                hi.append(g * EXPERTS_PER_GROUP + b)
    return np.asarray(lo, np.int32), np.asarray(hi, np.int32)


_CLASS_LO, _CLASS_HI = _class_tables()


def _pick_tile(n, candidates):
    for c in candidates:
        if n % c == 0:
            return c
    raise ValueError(f"no tile in {candidates} divides {n}")


def _params(*semantics):
    return pltpu.CompilerParams(dimension_semantics=semantics, vmem_limit_bytes=VMEM_LIMIT)


def _rms(x, g):
    return x * lax.rsqrt(jnp.mean(x * x, axis=-1, keepdims=True) + RMS_EPS) * g


EMBED_TILES = 3


def _embed_qkv_kernel(front_ref, xa_ref, xb_ref, xc_ref, g_ref, w_ref, h_ref, q_ref, k_ref, v_ref):
    first = jnp.where(pl.program_id(1) == 0, front_ref[...], xa_ref[0])
    h = jnp.concatenate([first, xb_ref[0], xc_ref[0]], axis=0)
    h_ref[...] = h
    _qkv_body(h, g_ref, w_ref, q_ref, k_ref, v_ref)


def _qkv_body(h, g_ref, w_ref, q_ref, k_ref, v_ref):
    xn = _rms(h, g_ref[...]).astype(_BF16)
    scale = HEAD_DIM ** -0.5 * LOG2_E
    q = jnp.dot(xn, w_ref[:, 0:D_MODEL], preferred_element_type=_F32)
    q_ref[...] = (q * scale).astype(q_ref.dtype)
    k = jnp.dot(xn, w_ref[:, D_MODEL:2 * D_MODEL], preferred_element_type=_F32)
    k_ref[...] = k.astype(k_ref.dtype)
    v = jnp.dot(xn, w_ref[:, 2 * D_MODEL:3 * D_MODEL], preferred_element_type=_F32)
    v_ref[...] = v.astype(v_ref.dtype)


def _embed_qkv(x, meta_tokens, gain, w_qkv):
    b, seq, _ = x.shape
    p = seq + ATTN_TILE
    tile = EMBED_TILES * ATTN_TILE
    nt = p // tile
    front = jnp.concatenate([jnp.zeros((FRONT, D_MODEL), x.dtype), meta_tokens.astype(x.dtype)], axis=0)

    def x_tile(offset):
        return pl.BlockSpec((1, ATTN_TILE, D_MODEL),
                            lambda bi, i: (bi, jnp.maximum(EMBED_TILES * i + offset - 1, 0), 0))

    row = pl.BlockSpec((tile, D_MODEL), lambda bi, i: (bi * nt + i, 0))
    out = jax.ShapeDtypeStruct((b * p, D_MODEL), _BF16)
    return pl.pallas_call(
        _embed_qkv_kernel,
        grid=(b, nt),
        in_specs=[_const_spec((ATTN_TILE, D_MODEL)), x_tile(0), x_tile(1), x_tile(2),
                  _const_spec((1, D_MODEL)), _const_spec((D_MODEL, 3 * D_MODEL))],
        out_specs=[row, row, row, row],
        out_shape=[jax.ShapeDtypeStruct((b * p, D_MODEL), _F32), out, out, out],
        compiler_params=_params("parallel", "parallel"),
        name="embed_qkv_proj",
    )(front, x, x, x, gain, w_qkv)


def _neg_abs(x):
    bits = lax.bitcast_convert_type(x, jnp.uint32) | jnp.uint32(0x80000000)
    return lax.bitcast_convert_type(bits, _F32)


def _attn_kernel(q_ref, k_ref, v_ref, o_ref, acc_ref, carry_ref):
    qi = pl.program_id(2)
    tri_r = lax.broadcasted_iota(jnp.int32, (ATTN_TILE, ATTN_TILE), 0)
    tri_c = lax.broadcasted_iota(jnp.int32, (ATTN_TILE, ATTN_TILE), 1)
    neg_suffix = jnp.where(tri_r > tri_c, -1.0, 0.0).astype(_BF16)
    acc_ref[...] = jnp.zeros_like(acc_ref)
    carry_ref[...] = jnp.zeros_like(carry_ref)

    def step(kb, masked):
        start = pl.multiple_of(kb * ATTN_TILE, ATTN_TILE)
        if masked:
            q_pos = qi * ATTN_TILE + tri_r
            k_pos = kb * ATTN_TILE + tri_c
            mask = (k_pos < q_pos) & (k_pos >= FRONT)
        for g in range(HEADS_PER_STEP):
            cols = slice(g * HEAD_DIM, (g + 1) * HEAD_DIM)
            k = k_ref[0, pl.ds(start, ATTN_TILE), cols]
            v = v_ref[0, pl.ds(start, ATTN_TILE), cols]
            z = lax.dot_general(q_ref[0, :, cols], k, (((1,), (1,)), ((), ())), preferred_element_type=_F32)
            sp = jnp.maximum(z, 0.0) + jnp.log(1.0 + jnp.exp2(_neg_abs(z))) * LOG2_E
            if masked:
                sp = jnp.where(mask, sp, 0.0)
            within = jnp.dot(sp.astype(_BF16), neg_suffix, preferred_element_type=_F32)
            later = within + carry_ref[g]
            a = jnp.exp2(z - sp + later)
            if masked:
                a = jnp.where(mask, a, 0.0)
            acc_ref[g] += jnp.dot(a.astype(_BF16), v, preferred_element_type=_F32)
            carry_ref[g] = later[:, 0:1] - sp[:, 0:1]

    def alive():
        top = carry_ref[0]
        for g in range(1, HEADS_PER_STEP):
            top = jnp.maximum(top, carry_ref[g])
        return (jnp.max(top) > EXHAUSTED_LOG2).astype(jnp.int32)

    step(qi, True)

    def cond(state):
        kb, live = state
        return jnp.logical_and(kb >= 1, live > 0)

    def body(state):
        kb, _ = state
        step(kb, False)
        return kb - 1, alive()

    kb_end, live = lax.while_loop(cond, body, (qi - 1, alive()))

    @pl.when(jnp.logical_and(kb_end == 0, live > 0))
    def _():
        step(0, True)

    for g in range(HEADS_PER_STEP):
        o_ref[0, :, g * HEAD_DIM:(g + 1) * HEAD_DIM] = acc_ref[g].astype(o_ref.dtype)


def _attention(q, k, v):
    b, p, _ = q.shape
    nq = p // ATTN_TILE
    width = HEADS_PER_STEP * HEAD_DIM
    qspec = pl.BlockSpec((1, ATTN_TILE, width), lambda bi, hi, qi: (bi, qi, hi))
    kvspec = pl.BlockSpec((1, p, width), lambda bi, hi, qi: (bi, 0, hi), pipeline_mode=pl.Buffered(1))
    return pl.pallas_call(
        _attn_kernel,
        grid=(b, HEADS // HEADS_PER_STEP, nq),
        in_specs=[qspec, kvspec, kvspec],
        out_specs=qspec,
        out_shape=jax.ShapeDtypeStruct(q.shape, _BF16),
        scratch_shapes=[pltpu.VMEM((HEADS_PER_STEP, ATTN_TILE, HEAD_DIM), _F32),
                        pltpu.VMEM((HEADS_PER_STEP, ATTN_TILE, 1), _F32)],
        compiler_params=_params("parallel", "parallel", "arbitrary"),
        name="sb_attention",
    )(q, k, v)


def _route(h_new, g_ref, wr_ref, br_ref, cnt_ref, xg_ref, counts_ref):
    tm = h_new.shape[0]
    xn = _rms(h_new, g_ref[...])
    x_hi = xn.astype(_BF16)
    x_lo = (xn - x_hi.astype(_F32)).astype(_BF16)
    both = jnp.dot(x_hi, wr_ref[...], preferred_element_type=_F32)
    logits = (both[:, 0:LANES] + both[:, LANES:2 * LANES]
              + jnp.dot(x_lo, wr_ref[:, 0:LANES], preferred_element_type=_F32)) + br_ref[...]
    lane = lax.broadcasted_iota(jnp.int32, (tm, LANES), 1).astype(_F32)

    def first_argmax(vals, vmax):
        return jnp.min(jnp.where(vals >= vmax, lane, float(LANES)), axis=-1, keepdims=True)

    gl = jnp.where(lane < MOE_GROUPS, logits, NEG_BIG)
    gmax = jnp.max(gl, axis=-1, keepdims=True)
    gsel = first_argmax(gl, gmax)
    gprob = 1.0 / jnp.sum(jnp.exp(gl - gmax), axis=-1, keepdims=True)
    first = MOE_GROUPS + EXPERTS_PER_GROUP * gsel
    el = jnp.where((lane >= first) & (lane < first + EXPERTS_PER_GROUP), logits, NEG_BIG)
    m1 = jnp.max(el, axis=-1, keepdims=True)
    i1 = first_argmax(el, m1)
    el2 = jnp.where(lane == i1, NEG_BIG, el)
    m2 = jnp.max(el2, axis=-1, keepdims=True)
    i2 = first_argmax(el2, m2)
    ratio = jnp.exp(m2 - m1)
    gate1 = gprob / (1.0 + ratio)
    gate2 = gprob * ratio / (1.0 + ratio)
    a = jnp.minimum(i1, i2) - first
    bb = jnp.maximum(i1, i2) - first
    pair = (EXPERTS_PER_GROUP - 1) * a - a * (a - 1.0) * 0.5 + (bb - a - 1.0)
    cls = gsel * PAIRS_PER_GROUP + pair
    gate_lo = jnp.where(i1 < i2, gate1, gate2)
    gate_hi = jnp.where(i1 < i2, gate2, gate1)

    onehot = (lane == cls).astype(_F32)
    tri_r = lax.broadcasted_iota(jnp.int32, (tm, tm), 0)
    tri_c = lax.broadcasted_iota(jnp.int32, (tm, tm), 1)
    before = (tri_c < tri_r).astype(_BF16)
    seen = jnp.dot(before, onehot.astype(_BF16), preferred_element_type=_F32) + cnt_ref[...]
    rank = jnp.sum(onehot * seen, axis=-1, keepdims=True)
    cnt_ref[...] += jnp.sum(onehot, axis=0, keepdims=True)
    counts_ref[...] = cnt_ref[...]

    extra = jnp.where(lane == GATE_LO_LANE, gate_lo,
                      jnp.where(lane == GATE_HI_LANE, gate_hi,
                                jnp.where(lane == CLASS_LANE, cls,
                                          jnp.where(lane == RANK_LANE, rank, 0.0))))
    low = lax.bitcast_convert_type(x_hi[:, 0:HALF].astype(_F32), _U32)
    high = lax.bitcast_convert_type(x_hi[:, HALF:D_MODEL].astype(_F32), _U32)
    xg_ref[:, 0:HALF] = high | (low >> 16)
    xg_ref[:, HALF:XG_WIDTH] = lax.bitcast_convert_type(extra, _U32)


def _router_weights(wg, bg, we, be):
    w = jnp.zeros((D_MODEL, LANES), _F32)
    w = w.at[:, 0:MOE_GROUPS].set(wg).at[:, MOE_GROUPS:MOE_GROUPS + N_EXPERTS].set(we)
    bias = jnp.zeros((1, LANES), _F32)
    bias = bias.at[0, 0:MOE_GROUPS].set(bg).at[0, MOE_GROUPS:MOE_GROUPS + N_EXPERTS].set(be)
    w_hi = w.astype(_BF16)
    w_lo = (w - w_hi.astype(_F32)).astype(_BF16)
    return jnp.concatenate([w_hi, w_lo], axis=1), bias


def _post_attn_kernel(o_ref, h_ref, wo_ref, g_ref, wr_ref, br_ref,
                      h_out_ref, xg_ref, counts_ref, cnt_ref):
    @pl.when(pl.program_id(0) == 0)
    def _():
        cnt_ref[...] = jnp.zeros_like(cnt_ref)

    h_new = h_ref[...] + jnp.dot(o_ref[...], wo_ref[...], preferred_element_type=_F32)
    h_out_ref[...] = h_new
    _route(h_new, g_ref, wr_ref, br_ref, cnt_ref, xg_ref, counts_ref)


def _const_spec(shape):
    return pl.BlockSpec(shape, lambda *_: (0,) * len(shape))


def _post_attn(o2, h2, w_o, gain, wr, br):
    t = h2.shape[0]
    tm = _pick_tile(t, (512, 256))
    row = pl.BlockSpec((tm, D_MODEL), lambda i: (i, 0))
    return pl.pallas_call(
        _post_attn_kernel,
        grid=(t // tm,),
        in_specs=[row, row, _const_spec((D_MODEL, D_MODEL)), _const_spec((1, D_MODEL)),
                  _const_spec((D_MODEL, 2 * LANES)), _const_spec((1, LANES))],
        out_specs=[row, pl.BlockSpec((tm, XG_WIDTH), lambda i: (i, 0)), _const_spec((1, LANES))],
        out_shape=[jax.ShapeDtypeStruct((t, D_MODEL), _F32), jax.ShapeDtypeStruct((t, XG_WIDTH), _U32),
                   jax.ShapeDtypeStruct((1, LANES), _F32)],
        scratch_shapes=[pltpu.VMEM((1, LANES), _F32)],
        compiler_params=_params("arbitrary"),
        name="attn_out_route",
    )(o2, h2, w_o, gain, wr, br)


def _pool_kernel(dest_ref, dest_next_ref, y_ref, h_ref, mg_ref, pw_ref, ps_ref, g_ref, wr_ref, br_ref,
                 h_out_ref, xg_ref, counts_ref, rows_ref, sems, cnt_ref, tail_ref, *, tile, tiles_per_seq):
    step = pl.program_id(0)
    i = step % tiles_per_seq

    @pl.when(step == 0)
    def _():
        cnt_ref[...] = jnp.zeros_like(cnt_ref)
        tail_ref[...] = jnp.zeros_like(tail_ref)

    rows, fetch_next = _expert_rows(dest_ref, dest_next_ref, y_ref, rows_ref, sems, tile)
    h = h_ref[...] + rows
    fetch_next()
    hn = _rms(h, mg_ref[...])
    halo = jnp.where(i == 0, 0.0, tail_ref[...])
    tail_ref[...] = hn[tile - POOL_HALO:, :]
    cat = jnp.concatenate([halo, hn], axis=0)
    token = i * tile + lax.broadcasted_iota(jnp.int32, (tile, 1), 0) - FRONT
    outs = []
    for gi, w in enumerate(POOL_WINDOWS):
        cols = slice(gi * POOL_GROUP_DIM, (gi + 1) * POOL_GROUP_DIM)
        xg = hn[:, cols]
        run = cat[:, cols]
        shift = 1
        while shift < w:
            run = run + pltpu.roll(run, shift, axis=0)
            shift *= 2
        wsum = run[POOL_HALO:, :]
        count = jnp.clip(token + 1, 1, w).astype(_F32)
        mixed = wsum / count - xg
        outs.append(jnp.dot(mixed.astype(_BF16), pw_ref[gi], preferred_element_type=_F32))
    h_new = h + jnp.concatenate(outs, axis=1) * ps_ref[...]
    h_out_ref[...] = h_new
    _route(h_new, g_ref, wr_ref, br_ref, cnt_ref, xg_ref, counts_ref)
    _expert_rows_finish(dest_ref, y_ref, rows_ref, sems, tile)


def _pool(y_sorted, dest, h2, seq_rows, mix_gain, pool_w, pool_scale, gain, wr, br):
    t = h2.shape[0]
    tile = _pick_tile(seq_rows, (768, 256))
    n = t // tile
    dest3 = dest.reshape(n, 1, tile)
    row = pl.BlockSpec((tile, D_MODEL), lambda s: (s, 0))
    return pl.pallas_call(
        functools.partial(_pool_kernel, tile=tile, tiles_per_seq=seq_rows // tile),
        grid=(n,),
        in_specs=_expert_rows_specs(n, tile, lambda s: s)
        + [row, _const_spec((1, D_MODEL)),
           _const_spec((len(POOL_WINDOWS), POOL_GROUP_DIM, POOL_GROUP_DIM)), _const_spec((1, D_MODEL)),
           _const_spec((1, D_MODEL)), _const_spec((D_MODEL, 2 * LANES)), _const_spec((1, LANES))],
        out_specs=[row, pl.BlockSpec((tile, XG_WIDTH), lambda s: (s, 0)), _const_spec((1, LANES))],
        out_shape=[jax.ShapeDtypeStruct(h2.shape, _F32), jax.ShapeDtypeStruct((t, XG_WIDTH), _U32),
                   jax.ShapeDtypeStruct((1, LANES), _F32)],
        scratch_shapes=_expert_rows_scratch(tile)
        + [pltpu.VMEM((1, LANES), _F32), pltpu.VMEM((POOL_HALO, D_MODEL), _F32)],
        input_output_aliases={3: 0},
        compiler_params=_params("arbitrary"),
        name="combine_pool_route",
    )(dest3, dest3, y_sorted, h2, mix_gain, pool_w, pool_scale, gain, wr, br)


def _issue_rows(tm, copy_for_row):
    def group(g, c):
        r0 = pl.multiple_of(g * SUBLANES, SUBLANES)
        for j in range(SUBLANES):
            copy_for_row(r0, j).start(priority=j % DMA_THREADS)
        return c

    lax.fori_loop(0, tm // SUBLANES, group, 0)


def _issue_rows_inline(tm, copy_for_row):
    for r0 in range(0, tm, SUBLANES):
        for j in range(SUBLANES):
            copy_for_row(r0, j).start(priority=j % DMA_THREADS)


def _tile_row(ref, r0, j):
    return ref.at[pl.ds(r0, SUBLANES)].at[pl.ds(j, 1)]


def _wait_rows(tm, copy):
    def wait(r, c):
        copy.wait()
        return c

    lax.fori_loop(0, tm, wait, 0, unroll=8)


def _dispatch_kernel(dest_ref, xg_ref, out_ref, sem, *, tm):
    def copy(r0, j):
        return pltpu.make_async_copy(_tile_row(xg_ref, r0, j), out_ref.at[pl.ds(dest_ref[0, 0, r0 + j], 1)], sem)

    _issue_rows_inline(tm, copy)
    _wait_rows(tm, copy(0, 0))


def _dispatch(xg, dest3):
    t = xg.shape[0]
    tm = dest3.shape[-1]
    return pl.pallas_call(
        functools.partial(_dispatch_kernel, tm=tm),
        grid=(t // tm,),
        in_specs=[pl.BlockSpec((1, 1, tm), lambda i: (i, 0, 0), memory_space=pltpu.SMEM),
                  pl.BlockSpec((tm, XG_WIDTH), lambda i: (i, 0))],
        out_specs=pl.BlockSpec(memory_space=pl.ANY),
        out_shape=jax.ShapeDtypeStruct(xg.shape, xg.dtype),
        scratch_shapes=[pltpu.SemaphoreType.DMA(())],
        compiler_params=_params("arbitrary"),
        name="moe_dispatch",
    )(dest3, xg)


def _expert_kernel(blk_ref, lo_ref, hi_ref, elo_ref, ehi_ref,
                   x_ref, wg_lo, wu_lo, wd_lo, wg_hi, wu_hi, wd_hi, y_ref):
    del elo_ref, ehi_ref
    j = pl.program_id(0)

    @pl.when(jnp.logical_or(j == 0, blk_ref[jnp.maximum(j - 1, 0)] != blk_ref[j]))
    def _():
        y_ref[...] = jnp.zeros_like(y_ref)

    lo = lo_ref[j]
    hi = hi_ref[j]

    @pl.when(hi > lo)
    def _():
        row = lax.broadcasted_iota(jnp.int32, (ROW_BLOCK, 1), 0)
        inside = (row >= lo) & (row < hi)
        words = x_ref[:, 0:HALF]
        low = lax.bitcast_convert_type(words << 16, _F32)
        high = lax.bitcast_convert_type(words & _U32(HIGH_HALF_MASK), _F32)
        xb = jnp.concatenate([low, high], axis=1).astype(_BF16)
        gates = lax.bitcast_convert_type(x_ref[:, HALF:XG_WIDTH], _F32)

        def expert(wg, wu, wd):
            gate = jnp.dot(xb, wg[0], preferred_element_type=_F32)
            up = jnp.dot(xb, wu[0], preferred_element_type=_F32)
            hid = gate * jax.nn.sigmoid(gate) * up
            return jnp.dot(hid.astype(_BF16), wd[0], preferred_element_type=_F32)

        g_lo = jnp.where(inside, gates[:, GATE_LO_LANE:GATE_LO_LANE + 1], 0.0)
        g_hi = jnp.where(inside, gates[:, GATE_HI_LANE:GATE_HI_LANE + 1], 0.0)
        y_ref[...] += expert(wg_lo, wu_lo, wd_lo) * g_lo + expert(wg_hi, wu_hi, wd_hi) * g_hi


def _experts(x_sorted, item_blk, item_lo, item_hi, item_elo, item_ehi, layer, w_gate, w_up, w_down):
    n_items = item_blk.shape[0]

    def w_spec(shape, which):
        def index(j, blk, lo, hi, elo, ehi):
            return (layer, (elo, ehi)[which][j], 0, 0)
        return pl.BlockSpec(shape, index)

    w_in = (None, 1, D_MODEL, EXPERT_FF)
    w_out = (None, 1, EXPERT_FF, D_MODEL)
    grid_spec = pltpu.PrefetchScalarGridSpec(
        num_scalar_prefetch=5,
        grid=(n_items,),
        in_specs=[pl.BlockSpec((ROW_BLOCK, XG_WIDTH), lambda j, blk, *_: (blk[j], 0)),
                  w_spec(w_in, 0), w_spec(w_in, 0), w_spec(w_out, 0),
                  w_spec(w_in, 1), w_spec(w_in, 1), w_spec(w_out, 1)],
        out_specs=pl.BlockSpec((ROW_BLOCK, D_MODEL), lambda j, blk, *_: (blk[j], 0)),
    )
    return pl.pallas_call(
        _expert_kernel,
        grid_spec=grid_spec,
        out_shape=jax.ShapeDtypeStruct((x_sorted.shape[0], D_MODEL), _F32),
        compiler_params=_params("arbitrary"),
        name="moe_experts",
    )(item_blk, item_lo, item_hi, item_elo, item_ehi, x_sorted, w_gate, w_up, w_down, w_gate, w_up, w_down)


def _expert_rows(dest_ref, dest_next_ref, y_ref, rows_ref, sems, tm, fetch_first=False):
    i = pl.program_id(0)
    slot = i % 2

    def row_copy(d_ref, s):
        def copy(r0, j):
            src = y_ref.at[pl.ds(d_ref[0, 0, r0 + j], 1)]
            return pltpu.make_async_copy(src, _tile_row(rows_ref.at[s], r0, j), sems.at[s])
        return copy

    @pl.when(i == 0)
    def _():
        _issue_rows(tm, row_copy(dest_ref, 0))

    if fetch_first:
        _issue_rows_inline(tm, row_copy(dest_next_ref, 1 - slot))
    _wait_rows(tm, row_copy(dest_ref, slot)(0, 0))
    if fetch_first:
        return rows_ref[slot], lambda: None
    return rows_ref[slot], lambda: _issue_rows_inline(tm, row_copy(dest_next_ref, 1 - slot))


def _expert_rows_finish(dest_ref, y_ref, rows_ref, sems, tm):
    i = pl.program_id(0)

    @pl.when(i == pl.num_programs(0) - 1)
    def _():
        other = 1 - i % 2
        src = y_ref.at[pl.ds(dest_ref[0, 0, 0], 1)]
        _wait_rows(tm, pltpu.make_async_copy(src, _tile_row(rows_ref.at[other], 0, 0), sems.at[other]))


def _expert_rows_specs(n_steps, tm, tile_of_step):
    def dest(offset):
        return pl.BlockSpec((1, 1, tm), lambda i: (tile_of_step(jnp.minimum(i + offset, n_steps - 1)), 0, 0),
                            memory_space=pltpu.SMEM)
    return [dest(0), dest(1), pl.BlockSpec(memory_space=pl.ANY)]


def _expert_rows_scratch(tm):
    return [pltpu.VMEM((2, tm, D_MODEL), _F32), pltpu.SemaphoreType.DMA((2,))]


def _qkv_combine_kernel(dest_ref, dest_next_ref, y_ref, h_ref, g_ref, w_ref,
                        h_out_ref, q_ref, k_ref, v_ref, rows_ref, sems, *, tm):
    rows, fetch_next = _expert_rows(dest_ref, dest_next_ref, y_ref, rows_ref, sems, tm)
    h = h_ref[...] + rows
    fetch_next()
    h_out_ref[...] = h
    _qkv_body(h, g_ref, w_ref, q_ref, k_ref, v_ref)
    _expert_rows_finish(dest_ref, y_ref, rows_ref, sems, tm)


def _qkv_combine(y_sorted, dest, h2, gain, w_qkv):
    t = h2.shape[0]
    tm = _pick_tile(t, (512, 256))
    n = t // tm
    dest3 = dest.reshape(n, 1, tm)
    row = pl.BlockSpec((tm, D_MODEL), lambda i: (i, 0))
    out = jax.ShapeDtypeStruct((t, D_MODEL), _BF16)
    return pl.pallas_call(
        functools.partial(_qkv_combine_kernel, tm=tm),
        grid=(n,),
        in_specs=_expert_rows_specs(n, tm, lambda s: s)
        + [row, _const_spec((1, D_MODEL)), _const_spec((D_MODEL, 3 * D_MODEL))],
        out_specs=[row, row, row, row],
        out_shape=[jax.ShapeDtypeStruct((t, D_MODEL), _F32), out, out, out],
        scratch_shapes=_expert_rows_scratch(tm),
        input_output_aliases={3: 0},
        compiler_params=_params("arbitrary"),
        name="combine_qkv_proj",
    )(dest3, dest3, y_sorted, h2, gain, w_qkv)


def _moe(xg, counts, layer, w_gate, w_up, w_down):
    t = xg.shape[0]
    tm = _pick_tile(t, (1024, 512, 256))
    n_blocks = t // ROW_BLOCK
    cnt = counts[0, :N_CLASSES].astype(jnp.int32)
    ends = jnp.cumsum(cnt)
    starts = ends - cnt
    cuts = jnp.sort(jnp.concatenate([jnp.arange(n_blocks, dtype=jnp.int32) * ROW_BLOCK, starts[1:]]))
    item_blk = jnp.minimum(cuts // ROW_BLOCK, n_blocks - 1)
    item_lo = cuts - item_blk * ROW_BLOCK
    item_hi = jnp.concatenate([cuts[1:], jnp.full((1,), t, jnp.int32)]) - item_blk * ROW_BLOCK
    class_ids = jnp.arange(N_CLASSES, dtype=jnp.int32)

    def lookup(table, idx):
        return jnp.sum(jnp.where(idx[:, None] == class_ids[None, :], table[None, :], 0), axis=1)

    item_cls = jnp.minimum(jnp.sum((ends[None, :] <= cuts[:, None]).astype(jnp.int32), axis=1), N_CLASSES - 1)
    item_elo = lookup(jnp.asarray(_CLASS_LO), item_cls)
    item_ehi = lookup(jnp.asarray(_CLASS_HI), item_cls)
    cls = lax.bitcast_convert_type(xg[:, HALF + CLASS_LANE], _F32).astype(jnp.int32)
    rank = lax.bitcast_convert_type(xg[:, HALF + RANK_LANE], _F32).astype(jnp.int32)
    start_of = lookup(starts, cls)
    dest = (start_of + rank).astype(jnp.int32)
    x_sorted = _dispatch(xg, dest.reshape(t // tm, 1, tm))
    y_sorted = _experts(x_sorted, item_blk, item_lo, item_hi, item_elo, item_ehi, layer, w_gate, w_up, w_down)
    return y_sorted, dest


def _final_kernel(dest_ref, dest_next_ref, y_ref, h_ref, g_ref, o_ref, rows_ref, sems, *, tm):
    rows, _ = _expert_rows(dest_ref, dest_next_ref, y_ref, rows_ref, sems, tm, fetch_first=True)
    o_ref[...] = _rms(h_ref[...] + rows, g_ref[...])
    _expert_rows_finish(dest_ref, y_ref, rows_ref, sems, tm)


def _final(y_sorted, dest, h2, gain, b, seq):
    tm = ATTN_TILE
    per_batch = seq // tm
    n = b * per_batch

    def tile_of_step(s):
        return (s // per_batch) * (per_batch + 1) + s % per_batch + 1

    out = pl.pallas_call(
        functools.partial(_final_kernel, tm=tm),
        grid=(n,),
        in_specs=_expert_rows_specs(n, tm, tile_of_step)
        + [pl.BlockSpec((tm, D_MODEL), lambda i: (tile_of_step(i), 0)), _const_spec((1, D_MODEL))],
        out_specs=pl.BlockSpec((tm, D_MODEL), lambda i: (i, 0)),
        out_shape=jax.ShapeDtypeStruct((b * seq, D_MODEL), _F32),
        scratch_shapes=_expert_rows_scratch(tm),
        compiler_params=_params("arbitrary"),
        name="final_norm",
    )(dest.reshape(-1, 1, tm), dest.reshape(-1, 1, tm), y_sorted, h2, gain)
    return out.reshape(b, seq, D_MODEL)


def kernel(x, meta_tokens, mix_norm, ffn_norm, final_norm, sb_w_qkv, sb_w_o, pool_w, pool_scale,
           router_group_w, router_group_b, router_expert_w, router_expert_b,
           expert_w_gate, expert_w_up, expert_w_down):
    b, seq, d = x.shape
    p = seq + ATTN_TILE
    assert d == D_MODEL and seq % ATTN_TILE == 0 and p % (EMBED_TILES * ATTN_TILE) == 0
    depth = mix_norm.shape[0]
    t = b * p
    w_gate, w_up, w_down = (w.astype(_BF16) for w in (expert_w_gate, expert_w_up, expert_w_down))
    h2 = None
    moe_out = None

    for i in range(depth):
        j = i // 2
        wr, br = _router_weights(router_group_w[i], router_group_b[i], router_expert_w[i], router_expert_b[i])
        ffn_gain = ffn_norm[i].reshape(1, d)
        mix_gain = mix_norm[i].reshape(1, d)
        if i % 2 == 0:
            w_qkv = sb_w_qkv[j].astype(_BF16)
            if i == 0:
                h2, q, k, v = _embed_qkv(x, meta_tokens, mix_gain, w_qkv)
            else:
                h2, q, k, v = _qkv_combine(*moe_out, h2, mix_gain, w_qkv)
            o = _attention(q.reshape(b, p, d), k.reshape(b, p, d), v.reshape(b, p, d))
            h2, xg, counts = _post_attn(o.reshape(t, d), h2, sb_w_o[j].astype(_BF16), ffn_gain, wr, br)
        else:
            assert moe_out is not None
            h2, xg, counts = _pool(*moe_out, h2, p, mix_gain, pool_w[j].astype(_BF16),
                                   pool_scale[j].reshape(1, d), ffn_gain, wr, br)
        moe_out = _moe(xg, counts, i, w_gate, w_up, w_down)
    return _final(*moe_out, h2, final_norm.reshape(1, d), b, seq)
```

```python
import functools

import numpy as np
import jax
import jax.numpy as jnp
from jax import lax
from jax.experimental import pallas as pl
from jax.experimental.pallas import tpu as pltpu

D_MODEL = 1024
N_META = 16
HEADS = 8
HEAD_DIM = D_MODEL // HEADS
POOL_WINDOWS = (2, 4, 8, 16)
POOL_GROUP_DIM = D_MODEL // len(POOL_WINDOWS)
POOL_HALO = 16
MOE_GROUPS = 4
EXPERTS_PER_GROUP = 8
N_EXPERTS = MOE_GROUPS * EXPERTS_PER_GROUP
EXPERT_FF = D_MODEL // 4
PAIRS_PER_GROUP = EXPERTS_PER_GROUP * (EXPERTS_PER_GROUP - 1) // 2
N_CLASSES = MOE_GROUPS * PAIRS_PER_GROUP
RMS_EPS = 1e-6

LANES = 128
SUBLANES = 8
DMA_THREADS = 2
ATTN_TILE = 256
FRONT = ATTN_TILE - N_META
HEADS_PER_STEP = 8
LOG2_E = 1.4426950408889634
EXHAUSTED_LOG2 = -150.0
ROW_BLOCK = 256
HALF = D_MODEL // 2
XG_WIDTH = HALF + LANES
_U32 = jnp.uint32
HIGH_HALF_MASK = 0xFFFF0000
GATE_LO_LANE, GATE_HI_LANE, CLASS_LANE, RANK_LANE = 0, 1, 2, 3
NEG_BIG = -1e30
VMEM_LIMIT = 48 * 1024 * 1024

_F32 = jnp.float32
_BF16 = jnp.bfloat16


def _class_tables():
    lo, hi = [], []
    for g in range(MOE_GROUPS):
        for a in range(EXPERTS_PER_GROUP):
            for b in range(a + 1, EXPERTS_PER_GROUP):
                lo.append(g * EXPERTS_PER_GROUP + a)
                hi.append(g * EXPERTS_PER_GROUP + b)
    return np.asarray(lo, np.int32), np.asarray(hi, np.int32)


_CLASS_LO, _CLASS_HI = _class_tables()


def _pick_tile(n, candidates):
    for c in candidates:
        if n % c == 0:
            return c
    raise ValueError(f"no tile in {candidates} divides {n}")


def _params(*semantics):
    return pltpu.CompilerParams(dimension_semantics=semantics, vmem_limit_bytes=VMEM_LIMIT)


def _rms(x, g):
    return x * lax.rsqrt(jnp.mean(x * x, axis=-1, keepdims=True) + RMS_EPS) * g


EMBED_TILES = 3


def _embed_qkv_kernel(front_ref, xa_ref, xb_ref, xc_ref, g_ref, w_ref, h_ref, q_ref, k_ref, v_ref):
    first = jnp.where(pl.program_id(1) == 0, front_ref[...], xa_ref[0])
    h = jnp.concatenate([first, xb_ref[0], xc_ref[0]], axis=0)
    h_ref[...] = h
    _qkv_body(h, g_ref, w_ref, q_ref, k_ref, v_ref)


def _qkv_body(h, g_ref, w_ref, q_ref, k_ref, v_ref):
    xn = _rms(h, g_ref[...]).astype(_BF16)
    scale = HEAD_DIM ** -0.5 * LOG2_E
    q = jnp.dot(xn, w_ref[:, 0:D_MODEL], preferred_element_type=_F32)
    q_ref[...] = (q * scale).astype(q_ref.dtype)
    k = jnp.dot(xn, w_ref[:, D_MODEL:2 * D_MODEL], preferred_element_type=_F32)
    k_ref[...] = k.astype(k_ref.dtype)
    v = jnp.dot(xn, w_ref[:, 2 * D_MODEL:3 * D_MODEL], preferred_element_type=_F32)
    v_ref[...] = v.astype(v_ref.dtype)


def _embed_qkv(x, meta_tokens, gain, w_qkv):
    b, seq, _ = x.shape
    p = seq + ATTN_TILE
    tile = EMBED_TILES * ATTN_TILE
    nt = p // tile
    front = jnp.concatenate([jnp.zeros((FRONT, D_MODEL), x.dtype), meta_tokens.astype(x.dtype)], axis=0)

    def x_tile(offset):
        return pl.BlockSpec((1, ATTN_TILE, D_MODEL),
                            lambda bi, i: (bi, jnp.maximum(EMBED_TILES * i + offset - 1, 0), 0))

    row = pl.BlockSpec((tile, D_MODEL), lambda bi, i: (bi * nt + i, 0))
    out = jax.ShapeDtypeStruct((b * p, D_MODEL), _BF16)
    return pl.pallas_call(
        _embed_qkv_kernel,
        grid=(b, nt),
        in_specs=[_const_spec((ATTN_TILE, D_MODEL)), x_tile(0), x_tile(1), x_tile(2),
                  _const_spec((1, D_MODEL)), _const_spec((D_MODEL, 3 * D_MODEL))],
        out_specs=[row, row, row, row],
        out_shape=[jax.ShapeDtypeStruct((b * p, D_MODEL), _F32), out, out, out],
        compiler_params=_params("parallel", "parallel"),
        name="embed_qkv_proj",
    )(front, x, x, x, gain, w_qkv)


def _neg_abs(x):
    bits = lax.bitcast_convert_type(x, jnp.uint32) | jnp.uint32(0x80000000)
    return lax.bitcast_convert_type(bits, _F32)


def _attn_kernel(q_ref, k_ref, v_ref, o_ref, acc_ref, carry_ref):
    qi = pl.program_id(2)
    tri_r = lax.broadcasted_iota(jnp.int32, (ATTN_TILE, ATTN_TILE), 0)
    tri_c = lax.broadcasted_iota(jnp.int32, (ATTN_TILE, ATTN_TILE), 1)
    neg_suffix = jnp.where(tri_r > tri_c, -1.0, 0.0).astype(_BF16)
    acc_ref[...] = jnp.zeros_like(acc_ref)
    carry_ref[...] = jnp.zeros_like(carry_ref)

    def step(kb, masked):
        start = pl.multiple_of(kb * ATTN_TILE, ATTN_TILE)
        if masked:
            q_pos = qi * ATTN_TILE + tri_r
            k_pos = kb * ATTN_TILE + tri_c
            mask = (k_pos < q_pos) & (k_pos >= FRONT)
        for g in range(HEADS_PER_STEP):
            cols = slice(g * HEAD_DIM, (g + 1) * HEAD_DIM)
            k = k_ref[0, pl.ds(start, ATTN_TILE), cols]
            v = v_ref[0, pl.ds(start, ATTN_TILE), cols]
            z = lax.dot_general(q_ref[0, :, cols], k, (((1,), (1,)), ((), ())), preferred_element_type=_F32)
            sp = jnp.maximum(z, 0.0) + jnp.log(1.0 + jnp.exp2(_neg_abs(z))) * LOG2_E
            if masked:
                sp = jnp.where(mask, sp, 0.0)
            within = jnp.dot(sp.astype(_BF16), neg_suffix, preferred_element_type=_F32)
            later = within + carry_ref[g]
            a = jnp.exp2(z - sp + later)
            if masked:
                a = jnp.where(mask, a, 0.0)
            acc_ref[g] += jnp.dot(a.astype(_BF16), v, preferred_element_type=_F32)
            carry_ref[g] = later[:, 0:1] - sp[:, 0:1]

    def alive():
        top = carry_ref[0]
        for g in range(1, HEADS_PER_STEP):
            top = jnp.maximum(top, carry_ref[g])
        return (jnp.max(top) > EXHAUSTED_LOG2).astype(jnp.int32)

    step(qi, True)

    def cond(state):
        kb, live = state
        return jnp.logical_and(kb >= 1, live > 0)

    def body(state):
        kb, _ = state
        step(kb, False)
        return kb - 1, alive()

    kb_end, live = lax.while_loop(cond, body, (qi - 1, alive()))

    @pl.when(jnp.logical_and(kb_end == 0, live > 0))
    def _():
        step(0, True)

    for g in range(HEADS_PER_STEP):
        o_ref[0, :, g * HEAD_DIM:(g + 1) * HEAD_DIM] = acc_ref[g].astype(o_ref.dtype)


def _attention(q, k, v):
    b, p, _ = q.shape
    nq = p // ATTN_TILE
    width = HEADS_PER_STEP * HEAD_DIM
    qspec = pl.BlockSpec((1, ATTN_TILE, width), lambda bi, hi, qi: (bi, qi, hi))
    kvspec = pl.BlockSpec((1, p, width), lambda bi, hi, qi: (bi, 0, hi), pipeline_mode=pl.Buffered(1))
    return pl.pallas_call(
        _attn_kernel,
        grid=(b, HEADS // HEADS_PER_STEP, nq),
        in_specs=[qspec, kvspec, kvspec],
        out_specs=qspec,
        out_shape=jax.ShapeDtypeStruct(q.shape, _BF16),
        scratch_shapes=[pltpu.VMEM((HEADS_PER_STEP, ATTN_TILE, HEAD_DIM), _F32),
                        pltpu.VMEM((HEADS_PER_STEP, ATTN_TILE, 1), _F32)],
        compiler_params=_params("parallel", "parallel", "arbitrary"),
        name="sb_attention",
    )(q, k, v)


def _route(h_new, g_ref, wr_ref, br_ref, cnt_ref, xg_ref, counts_ref):
    tm = h_new.shape[0]
    xn = _rms(h_new, g_ref[...])
    x_hi = xn.astype(_BF16)
    x_lo = (xn - x_hi.astype(_F32)).astype(_BF16)
    both = jnp.dot(x_hi, wr_ref[...], preferred_element_type=_F32)
    logits = (both[:, 0:LANES] + both[:, LANES:2 * LANES]
              + jnp.dot(x_lo, wr_ref[:, 0:LANES], preferred_element_type=_F32)) + br_ref[...]
    lane = lax.broadcasted_iota(jnp.int32, (tm, LANES), 1).astype(_F32)

    def first_argmax(vals, vmax):
        return jnp.min(jnp.where(vals >= vmax, lane, float(LANES)), axis=-1, keepdims=True)

    gl = jnp.where(lane < MOE_GROUPS, logits, NEG_BIG)
    gmax = jnp.max(gl, axis=-1, keepdims=True)
    gsel = first_argmax(gl, gmax)
    gprob = 1.0 / jnp.sum(jnp.exp(gl - gmax), axis=-1, keepdims=True)
    first = MOE_GROUPS + EXPERTS_PER_GROUP * gsel
    el = jnp.where((lane >= first) & (lane < first + EXPERTS_PER_GROUP), logits, NEG_BIG)
    m1 = jnp.max(el, axis=-1, keepdims=True)
    i1 = first_argmax(el, m1)
    el2 = jnp.where(lane == i1, NEG_BIG, el)
    m2 = jnp.max(el2, axis=-1, keepdims=True)
    i2 = first_argmax(el2, m2)
    ratio = jnp.exp(m2 - m1)
    gate1 = gprob / (1.0 + ratio)
    gate2 = gprob * ratio / (1.0 + ratio)
    a = jnp.minimum(i1, i2) - first
    bb = jnp.maximum(i1, i2) - first
    pair = (EXPERTS_PER_GROUP - 1) * a - a * (a - 1.0) * 0.5 + (bb - a - 1.0)
    cls = gsel * PAIRS_PER_GROUP + pair
    gate_lo = jnp.where(i1 < i2, gate1, gate2)
    gate_hi = jnp.where(i1 < i2, gate2, gate1)

    onehot = (lane == cls).astype(_F32)
    tri_r = lax.broadcasted_iota(jnp.int32, (tm, tm), 0)
    tri_c = lax.broadcasted_iota(jnp.int32, (tm, tm), 1)
    before = (tri_c < tri_r).astype(_BF16)
    seen = jnp.dot(before, onehot.astype(_BF16), preferred_element_type=_F32) + cnt_ref[...]
    rank = jnp.sum(onehot * seen, axis=-1, keepdims=True)
    cnt_ref[...] += jnp.sum(onehot, axis=0, keepdims=True)
    counts_ref[...] = cnt_ref[...]

    extra = jnp.where(lane == GATE_LO_LANE, gate_lo,
                      jnp.where(lane == GATE_HI_LANE, gate_hi,
                                jnp.where(lane == CLASS_LANE, cls,
                                          jnp.where(lane == RANK_LANE, rank, 0.0))))
    low = lax.bitcast_convert_type(x_hi[:, 0:HALF].astype(_F32), _U32)
    high = lax.bitcast_convert_type(x_hi[:, HALF:D_MODEL].astype(_F32), _U32)
    xg_ref[:, 0:HALF] = high | (low >> 16)
    xg_ref[:, HALF:XG_WIDTH] = lax.bitcast_convert_type(extra, _U32)


def _router_weights(wg, bg, we, be):
    w = jnp.zeros((D_MODEL, LANES), _F32)
    w = w.at[:, 0:MOE_GROUPS].set(wg).at[:, MOE_GROUPS:MOE_GROUPS + N_EXPERTS].set(we)
    bias = jnp.zeros((1, LANES), _F32)
    bias = bias.at[0, 0:MOE_GROUPS].set(bg).at[0, MOE_GROUPS:MOE_GROUPS + N_EXPERTS].set(be)
    w_hi = w.astype(_BF16)
    w_lo = (w - w_hi.astype(_F32)).astype(_BF16)
    return jnp.concatenate([w_hi, w_lo], axis=1), bias


def _post_attn_kernel(o_ref, h_ref, wo_ref, g_ref, wr_ref, br_ref,
                      h_out_ref, xg_ref, counts_ref, cnt_ref):
    @pl.when(pl.program_id(0) == 0)
    def _():
        cnt_ref[...] = jnp.zeros_like(cnt_ref)

    h_new = h_ref[...] + jnp.dot(o_ref[...], wo_ref[...], preferred_element_type=_F32)
    h_out_ref[...] = h_new
    _route(h_new, g_ref, wr_ref, br_ref, cnt_ref, xg_ref, counts_ref)


def _const_spec(shape):
    return pl.BlockSpec(shape, lambda *_: (0,) * len(shape))


def _post_attn(o2, h2, w_o, gain, wr, br):
    t = h2.shape[0]
    tm = _pick_tile(t, (512, 256))
    row = pl.BlockSpec((tm, D_MODEL), lambda i: (i, 0))
    return pl.pallas_call(
        _post_attn_kernel,
        grid=(t // tm,),
        in_specs=[row, row, _const_spec((D_MODEL, D_MODEL)), _const_spec((1, D_MODEL)),
                  _const_spec((D_MODEL, 2 * LANES)), _const_spec((1, LANES))],
        out_specs=[row, pl.BlockSpec((tm, XG_WIDTH), lambda i: (i, 0)), _const_spec((1, LANES))],
        out_shape=[jax.ShapeDtypeStruct((t, D_MODEL), _F32), jax.ShapeDtypeStruct((t, XG_WIDTH), _U32),
                   jax.ShapeDtypeStruct((1, LANES), _F32)],
        scratch_shapes=[pltpu.VMEM((1, LANES), _F32)],
        compiler_params=_params("arbitrary"),
        name="attn_out_route",
    )(o2, h2, w_o, gain, wr, br)


def _pool_kernel(dest_ref, dest_next_ref, y_ref, h_ref, mg_ref, pw_ref, ps_ref, g_ref, wr_ref, br_ref,
                 h_out_ref, xg_ref, counts_ref, rows_ref, sems, cnt_ref, tail_ref, *, tile, tiles_per_seq):
    step = pl.program_id(0)
    i = step % tiles_per_seq

    @pl.when(step == 0)
    def _():
        cnt_ref[...] = jnp.zeros_like(cnt_ref)
        tail_ref[...] = jnp.zeros_like(tail_ref)

    rows, fetch_next = _expert_rows(dest_ref, dest_next_ref, y_ref, rows_ref, sems, tile)
    h = h_ref[...] + rows
    fetch_next()
    hn = _rms(h, mg_ref[...])
    halo = jnp.where(i == 0, 0.0, tail_ref[...])
    tail_ref[...] = hn[tile - POOL_HALO:, :]
    cat = jnp.concatenate([halo, hn], axis=0)
    token = i * tile + lax.broadcasted_iota(jnp.int32, (tile, 1), 0) - FRONT
    outs = []
    for gi, w in enumerate(POOL_WINDOWS):
        cols = slice(gi * POOL_GROUP_DIM, (gi + 1) * POOL_GROUP_DIM)
        xg = hn[:, cols]
        run = cat[:, cols]
        shift = 1
        while shift < w:
            run = run + pltpu.roll(run, shift, axis=0)
            shift *= 2
        wsum = run[POOL_HALO:, :]
        count = jnp.clip(token + 1, 1, w).astype(_F32)
        mixed = wsum / count - xg
        outs.append(jnp.dot(mixed.astype(_BF16), pw_ref[gi], preferred_element_type=_F32))
    h_new = h + jnp.concatenate(outs, axis=1) * ps_ref[...]
    h_out_ref[...] = h_new
    _route(h_new, g_ref, wr_ref, br_ref, cnt_ref, xg_ref, counts_ref)
    _expert_rows_finish(dest_ref, y_ref, rows_ref, sems, tile)


def _pool(y_sorted, dest, h2, seq_rows, mix_gain, pool_w, pool_scale, gain, wr, br):
    t = h2.shape[0]
    tile = _pick_tile(seq_rows, (768, 256))
    n = t // tile
    dest3 = dest.reshape(n, 1, tile)
    row = pl.BlockSpec((tile, D_MODEL), lambda s: (s, 0))
    return pl.pallas_call(
        functools.partial(_pool_kernel, tile=tile, tiles_per_seq=seq_rows // tile),
        grid=(n,),
        in_specs=_expert_rows_specs(n, tile, lambda s: s)
        + [row, _const_spec((1, D_MODEL)),
           _const_spec((len(POOL_WINDOWS), POOL_GROUP_DIM, POOL_GROUP_DIM)), _const_spec((1, D_MODEL)),
           _const_spec((1, D_MODEL)), _const_spec((D_MODEL, 2 * LANES)), _const_spec((1, LANES))],
        out_specs=[row, pl.BlockSpec((tile, XG_WIDTH), lambda s: (s, 0)), _const_spec((1, LANES))],
        out_shape=[jax.ShapeDtypeStruct(h2.shape, _F32), jax.ShapeDtypeStruct((t, XG_WIDTH), _U32),
                   jax.ShapeDtypeStruct((1, LANES), _F32)],
        scratch_shapes=_expert_rows_scratch(tile)
        + [pltpu.VMEM((1, LANES), _F32), pltpu.VMEM((POOL_HALO, D_MODEL), _F32)],
        input_output_aliases={3: 0},
        compiler_params=_params("arbitrary"),
        name="combine_pool_route",
    )(dest3, dest3, y_sorted, h2, mix_gain, pool_w, pool_scale, gain, wr, br)


def _issue_rows(tm, copy_for_row):
    def group(g, c):
        r0 = pl.multiple_of(g * SUBLANES, SUBLANES)
        for j in range(SUBLANES):
            copy_for_row(r0, j).start(priority=j % DMA_THREADS)
        return c

    lax.fori_loop(0, tm // SUBLANES, group, 0)


def _issue_rows_inline(tm, copy_for_row):
    for r0 in range(0, tm, SUBLANES):
        for j in range(SUBLANES):
            copy_for_row(r0, j).start(priority=j % DMA_THREADS)


def _tile_row(ref, r0, j):
    return ref.at[pl.ds(r0, SUBLANES)].at[pl.ds(j, 1)]


def _wait_rows(tm, copy):
    def wait(r, c):
        copy.wait()
        return c

    lax.fori_loop(0, tm, wait, 0, unroll=8)


def _dispatch_kernel(dest_ref, xg_ref, out_ref, sem, *, tm):
    def copy(r0, j):
        return pltpu.make_async_copy(_tile_row(xg_ref, r0, j), out_ref.at[pl.ds(dest_ref[0, 0, r0 + j], 1)], sem)

    _issue_rows_inline(tm, copy)
    _wait_rows(tm, copy(0, 0))


def _dispatch(xg, dest3):
    t = xg.shape[0]
    tm = dest3.shape[-1]
    return pl.pallas_call(
        functools.partial(_dispatch_kernel, tm=tm),
        grid=(t // tm,),
        in_specs=[pl.BlockSpec((1, 1, tm), lambda i: (i, 0, 0), memory_space=pltpu.SMEM),
                  pl.BlockSpec((tm, XG_WIDTH), lambda i: (i, 0))],
        out_specs=pl.BlockSpec(memory_space=pl.ANY),
        out_shape=jax.ShapeDtypeStruct(xg.shape, xg.dtype),
        scratch_shapes=[pltpu.SemaphoreType.DMA(())],
        compiler_params=_params("arbitrary"),
        name="moe_dispatch",
    )(dest3, xg)


def _expert_kernel(blk_ref, lo_ref, hi_ref, elo_ref, ehi_ref,
                   x_ref, wg_lo, wu_lo, wd_lo, wg_hi, wu_hi, wd_hi, y_ref):
    del elo_ref, ehi_ref
    j = pl.program_id(0)

    @pl.when(jnp.logical_or(j == 0, blk_ref[jnp.maximum(j - 1, 0)] != blk_ref[j]))
    def _():
        y_ref[...] = jnp.zeros_like(y_ref)

    lo = lo_ref[j]
    hi = hi_ref[j]

    @pl.when(hi > lo)
    def _():
        row = lax.broadcasted_iota(jnp.int32, (ROW_BLOCK, 1), 0)
        inside = (row >= lo) & (row < hi)
        words = x_ref[:, 0:HALF]
        low = lax.bitcast_convert_type(words << 16, _F32)
        high = lax.bitcast_convert_type(words & _U32(HIGH_HALF_MASK), _F32)
        xb = jnp.concatenate([low, high], axis=1).astype(_BF16)
        gates = lax.bitcast_convert_type(x_ref[:, HALF:XG_WIDTH], _F32)

        def expert(wg, wu, wd):
            gate = jnp.dot(xb, wg[0], preferred_element_type=_F32)
            up = jnp.dot(xb, wu[0], preferred_element_type=_F32)
            hid = gate * jax.nn.sigmoid(gate) * up
            return jnp.dot(hid.astype(_BF16), wd[0], preferred_element_type=_F32)

        g_lo = jnp.where(inside, gates[:, GATE_LO_LANE:GATE_LO_LANE + 1], 0.0)
        g_hi = jnp.where(inside, gates[:, GATE_HI_LANE:GATE_HI_LANE + 1], 0.0)
        y_ref[...] += expert(wg_lo, wu_lo, wd_lo) * g_lo + expert(wg_hi, wu_hi, wd_hi) * g_hi


def _experts(x_sorted, item_blk, item_lo, item_hi, item_elo, item_ehi, layer, w_gate, w_up, w_down):
    n_items = item_blk.shape[0]

    def w_spec(shape, which):
        def index(j, blk, lo, hi, elo, ehi):
            return (layer, (elo, ehi)[which][j], 0, 0)
        return pl.BlockSpec(shape, index)

    w_in = (None, 1, D_MODEL, EXPERT_FF)
    w_out = (None, 1, EXPERT_FF, D_MODEL)
    grid_spec = pltpu.PrefetchScalarGridSpec(
        num_scalar_prefetch=5,
        grid=(n_items,),
        in_specs=[pl.BlockSpec((ROW_BLOCK, XG_WIDTH), lambda j, blk, *_: (blk[j], 0)),
                  w_spec(w_in, 0), w_spec(w_in, 0), w_spec(w_out, 0),
                  w_spec(w_in, 1), w_spec(w_in, 1), w_spec(w_out, 1)],
        out_specs=pl.BlockSpec((ROW_BLOCK, D_MODEL), lambda j, blk, *_: (blk[j], 0)),
    )
    return pl.pallas_call(
        _expert_kernel,
        grid_spec=grid_spec,
        out_shape=jax.ShapeDtypeStruct((x_sorted.shape[0], D_MODEL), _F32),
        compiler_params=_params("arbitrary"),
        name="moe_experts",
    )(item_blk, item_lo, item_hi, item_elo, item_ehi, x_sorted, w_gate, w_up, w_down, w_gate, w_up, w_down)


def _expert_rows(dest_ref, dest_next_ref, y_ref, rows_ref, sems, tm, fetch_first=False):
    i = pl.program_id(0)
    slot = i % 2

    def row_copy(d_ref, s):
        def copy(r0, j):
            src = y_ref.at[pl.ds(d_ref[0, 0, r0 + j], 1)]
            return pltpu.make_async_copy(src, _tile_row(rows_ref.at[s], r0, j), sems.at[s])
        return copy

    @pl.when(i == 0)
    def _():
        _issue_rows(tm, row_copy(dest_ref, 0))

    if fetch_first:
        _issue_rows_inline(tm, row_copy(dest_next_ref, 1 - slot))
    _wait_rows(tm, row_copy(dest_ref, slot)(0, 0))
    if fetch_first:
        return rows_ref[slot], lambda: None
    return rows_ref[slot], lambda: _issue_rows_inline(tm, row_copy(dest_next_ref, 1 - slot))


def _expert_rows_finish(dest_ref, y_ref, rows_ref, sems, tm):
    i = pl.program_id(0)

    @pl.when(i == pl.num_programs(0) - 1)
    def _():
        other = 1 - i % 2
        src = y_ref.at[pl.ds(dest_ref[0, 0, 0], 1)]
        _wait_rows(tm, pltpu.make_async_copy(src, _tile_row(rows_ref.at[other], 0, 0), sems.at[other]))


def _expert_rows_specs(n_steps, tm, tile_of_step):
    def dest(offset):
        return pl.BlockSpec((1, 1, tm), lambda i: (tile_of_step(jnp.minimum(i + offset, n_steps - 1)), 0, 0),
                            memory_space=pltpu.SMEM)
    return [dest(0), dest(1), pl.BlockSpec(memory_space=pl.ANY)]


def _expert_rows_scratch(tm):
    return [pltpu.VMEM((2, tm, D_MODEL), _F32), pltpu.SemaphoreType.DMA((2,))]


def _qkv_combine_kernel(dest_ref, dest_next_ref, y_ref, h_ref, g_ref, w_ref,
                        h_out_ref, q_ref, k_ref, v_ref, rows_ref, sems, *, tm):
    rows, fetch_next = _expert_rows(dest_ref, dest_next_ref, y_ref, rows_ref, sems, tm)
    h = h_ref[...] + rows
    fetch_next()
    h_out_ref[...] = h
    _qkv_body(h, g_ref, w_ref, q_ref, k_ref, v_ref)
    _expert_rows_finish(dest_ref, y_ref, rows_ref, sems, tm)


def _qkv_combine(y_sorted, dest, h2, gain, w_qkv):
    t = h2.shape[0]
    tm = _pick_tile(t, (512, 256))
    n = t // tm
    dest3 = dest.reshape(n, 1, tm)
    row = pl.BlockSpec((tm, D_MODEL), lambda i: (i, 0))
    out = jax.ShapeDtypeStruct((t, D_MODEL), _BF16)
    return pl.pallas_call(
        functools.partial(_qkv_combine_kernel, tm=tm),
        grid=(n,),
        in_specs=_expert_rows_specs(n, tm, lambda s: s)
        + [row, _const_spec((1, D_MODEL)), _const_spec((D_MODEL, 3 * D_MODEL))],
        out_specs=[row, row, row, row],
        out_shape=[jax.ShapeDtypeStruct((t, D_MODEL), _F32), out, out, out],
        scratch_shapes=_expert_rows_scratch(tm),
        input_output_aliases={3: 0},
        compiler_params=_params("arbitrary"),
        name="combine_qkv_proj",
    )(dest3, dest3, y_sorted, h2, gain, w_qkv)


def _moe(xg, counts, layer, w_gate, w_up, w_down):
    t = xg.shape[0]
    tm = _pick_tile(t, (2048, 1024, 512, 256))
    n_blocks = t // ROW_BLOCK
    cnt = counts[0, :N_CLASSES].astype(jnp.int32)
    ends = jnp.cumsum(cnt)
    starts = ends - cnt
    cuts = jnp.sort(jnp.concatenate([jnp.arange(n_blocks, dtype=jnp.int32) * ROW_BLOCK, starts[1:]]))
    item_blk = jnp.minimum(cuts // ROW_BLOCK, n_blocks - 1)
    item_lo = cuts - item_blk * ROW_BLOCK
    item_hi = jnp.concatenate([cuts[1:], jnp.full((1,), t, jnp.int32)]) - item_blk * ROW_BLOCK
    class_ids = jnp.arange(N_CLASSES, dtype=jnp.int32)

    def lookup(table, idx):
        return jnp.sum(jnp.where(idx[:, None] == class_ids[None, :], table[None, :], 0), axis=1)

    item_cls = jnp.minimum(jnp.sum((ends[None, :] <= cuts[:, None]).astype(jnp.int32), axis=1), N_CLASSES - 1)
    item_elo = lookup(jnp.asarray(_CLASS_LO), item_cls)
    item_ehi = lookup(jnp.asarray(_CLASS_HI), item_cls)
    cls = lax.bitcast_convert_type(xg[:, HALF + CLASS_LANE], _F32).astype(jnp.int32)
    rank = lax.bitcast_convert_type(xg[:, HALF + RANK_LANE], _F32).astype(jnp.int32)
    start_of = lookup(starts, cls)
    dest = (start_of + rank).astype(jnp.int32)
    x_sorted = _dispatch(xg, dest.reshape(t // tm, 1, tm))
    y_sorted = _experts(x_sorted, item_blk, item_lo, item_hi, item_elo, item_ehi, layer, w_gate, w_up, w_down)
    return y_sorted, dest


def _final_kernel(dest_ref, dest_next_ref, y_ref, h_ref, g_ref, o_ref, rows_ref, sems, *, tm):
    rows, _ = _expert_rows(dest_ref, dest_next_ref, y_ref, rows_ref, sems, tm, fetch_first=True)
    o_ref[...] = _rms(h_ref[...] + rows, g_ref[...])
    _expert_rows_finish(dest_ref, y_ref, rows_ref, sems, tm)


def _final(y_sorted, dest, h2, gain, b, seq):
    tm = ATTN_TILE
    per_batch = seq // tm
    n = b * per_batch

    def tile_of_step(s):
        return (s // per_batch) * (per_batch + 1) + s % per_batch + 1

    out = pl.pallas_call(
        functools.partial(_final_kernel, tm=tm),
        grid=(n,),
        in_specs=_expert_rows_specs(n, tm, tile_of_step)
        + [pl.BlockSpec((tm, D_MODEL), lambda i: (tile_of_step(i), 0)), _const_spec((1, D_MODEL))],
        out_specs=pl.BlockSpec((tm, D_MODEL), lambda i: (i, 0)),
        out_shape=jax.ShapeDtypeStruct((b * seq, D_MODEL), _F32),
        scratch_shapes=_expert_rows_scratch(tm),
        compiler_params=_params("arbitrary"),
        name="final_norm",
    )(dest.reshape(-1, 1, tm), dest.reshape(-1, 1, tm), y_sorted, h2, gain)
    return out.reshape(b, seq, D_MODEL)


def kernel(x, meta_tokens, mix_norm, ffn_norm, final_norm, sb_w_qkv, sb_w_o, pool_w, pool_scale,
           router_group_w, router_group_b, router_expert_w, router_expert_b,
           expert_w_gate, expert_w_up, expert_w_down):
    b, seq, d = x.shape
    p = seq + ATTN_TILE
    assert d == D_MODEL and seq % ATTN_TILE == 0 and p % (EMBED_TILES * ATTN_TILE) == 0
    depth = mix_norm.shape[0]
    t = b * p
    w_gate, w_up, w_down = (w.astype(_BF16) for w in (expert_w_gate, expert_w_up, expert_w_down))
    h2 = None
    moe_out = None

    for i in range(depth):
        j = i // 2
        wr, br = _router_weights(router_group_w[i], router_group_b[i], router_expert_w[i], router_expert_b[i])
        ffn_gain = ffn_norm[i].reshape(1, d)
        mix_gain = mix_norm[i].reshape(1, d)
        if i % 2 == 0:
            w_qkv = sb_w_qkv[j].astype(_BF16)
            if i == 0:
                h2, q, k, v = _embed_qkv(x, meta_tokens, mix_gain, w_qkv)
            else:
                h2, q, k, v = _qkv_combine(*moe_out, h2, mix_gain, w_qkv)
            o = _attention(q.reshape(b, p, d), k.reshape(b, p, d), v.reshape(b, p, d))
            h2, xg, counts = _post_attn(o.reshape(t, d), h2, sb_w_o[j].astype(_BF16), ffn_gain, wr, br)
        else:
            assert moe_out is not None
            h2, xg, counts = _pool(*moe_out, h2, p, mix_gain, pool_w[j].astype(_BF16),
                                   pool_scale[j].reshape(1, d), ffn_gain, wr, br)
        moe_out = _moe(xg, counts, i, w_gate, w_up, w_down)
    return _final(*moe_out, h2, final_norm.reshape(1, d), b, seq)
```
